```python
import jax, jax.numpy as jnp
from jax import lax
import numpy as np

D_MODEL = 4096
BATCH = 2
SEQ = 8192
DEPTH = 1

POOL_WINDOWS = (2, 4, 8, 16)
N_POOL_GROUPS = 4
POOL_GROUP_DIM = 512
POOL_WIDTH = N_POOL_GROUPS * POOL_GROUP_DIM
HG_HEADS = 16
HG_DIM = 128
HG_WIDTH = HG_HEADS * HG_DIM
HG_CHUNK = 64
IN_COLS = POOL_WIDTH + 4 * HG_WIDTH + 2 * D_MODEL
N_EXPERTS = 128
EXPERT_DIM = 256
SHARED_DIM = 256
TOP_K = 8
N_GROUPS = 8
TOPK_GROUPS = 4
ROUTED_SCALE = 2.5
MOE_BLOCK = 128
PLE_DIM = 256
LN_EPS = 1e-5
RMS_EPS = 1e-6

kernel_name = "pool_hgrn2_gated_moe_deepnorm_layer"


def layer_norm(x, g, b):
    xf = x.astype(jnp.float32)
    mu = jnp.mean(xf, axis=-1, keepdims=True)
    xc = xf - mu
    var = jnp.mean(xc * xc, axis=-1, keepdims=True)
    y = xc * lax.rsqrt(var + LN_EPS) * g.astype(jnp.float32) + b.astype(jnp.float32)
    return y.astype(x.dtype)


def multiscale_pool(u, pool_w, pool_scale):
    B, S, _ = u.shape
    ug = u.astype(jnp.float32).reshape(B, S, N_POOL_GROUPS, POOL_GROUP_DIM)
    cs = jnp.cumsum(ug, axis=1)
    t1 = jnp.arange(1, S + 1, dtype=jnp.float32)
    outs = []
    for g, w in enumerate(POOL_WINDOWS):
        c = cs[:, :, g]
        lag = jnp.concatenate([jnp.zeros((B, w, POOL_GROUP_DIM), jnp.float32), c[:, :S - w]], axis=1)
        mean = (c - lag) / jnp.minimum(t1, w)[None, :, None]
        outs.append(mean - ug[:, :, g])
    d = jnp.stack(outs, axis=2)
    y = jnp.einsum('bsgc,gcd->bsgd', d, pool_w.astype(jnp.float32))
    return (y.reshape(B, S, POOL_WIDTH) * pool_scale.astype(jnp.float32)).astype(u.dtype)


def hgrn2(q_raw, f_raw, i_raw, og_raw, lb, norm_w):
    B, S, _ = q_raw.shape
    shp = (B, S, HG_HEADS, HG_DIM)
    q = jax.nn.silu(q_raw.astype(jnp.float32)).reshape(shp)
    fl = f_raw.astype(jnp.float32)
    lbf = lb.astype(jnp.float32)
    log_f = jnp.logaddexp(jnp.log(lbf), jnp.log1p(-lbf) + jax.nn.log_sigmoid(fl)).reshape(shp)
    k = ((1.0 - lbf) * jax.nn.sigmoid(-fl)).reshape(shp)
    v = i_raw.astype(jnp.float32).reshape(shp)
    nc = S // HG_CHUNK

    def to_chunks(a):
        return a.reshape(B, nc, HG_CHUNK, HG_HEADS, HG_DIM).transpose(1, 0, 3, 2, 4)

    causal = jnp.tril(jnp.ones((HG_CHUNK, HG_CHUNK), dtype=bool))

    def step(state, inp):
        qc, kc, vc, gc = inp
        b = jnp.cumsum(gc, axis=2)
        o_inter = jnp.einsum('bhck,bhkv->bhcv', qc * jnp.exp(b), state)
        diff = b[:, :, :, None, :] - b[:, :, None, :, :]
        decay = jnp.exp(jnp.where(causal[None, None, :, :, None], diff, -jnp.inf))
        scores = jnp.einsum('bhtk,bhsk,bhtsk->bhts', qc, kc, decay)
        o_intra = jnp.einsum('bhts,bhsv->bhtv', scores, vc)
        b_last = b[:, :, -1:, :]
        k_dec = kc * jnp.exp(b_last - b)
        new_state = jnp.exp(b_last[:, :, 0, :])[..., None] * state + jnp.einsum('bhsk,bhsv->bhkv', k_dec, vc)
        return new_state, o_inter + o_intra

    s0 = jnp.zeros((B, HG_HEADS, HG_DIM, HG_DIM), jnp.float32)
    _, o = lax.scan(step, s0, (to_chunks(q), to_chunks(k), to_chunks(v), to_chunks(log_f)))
    o = o.transpose(1, 0, 3, 2, 4).reshape(B, S, HG_HEADS, HG_DIM)
    o = o * lax.rsqrt(jnp.mean(o * o, axis=-1, keepdims=True) + RMS_EPS)
    o = o.reshape(B, S, HG_WIDTH) * norm_w.astype(jnp.float32) * jax.nn.silu(og_raw.astype(jnp.float32))
    return o.astype(q_raw.dtype)


def route(xt, router_w, router_bias):
    T = xt.shape[0]
    scores = jax.nn.sigmoid(xt.astype(jnp.float32) @ router_w.astype(jnp.float32))
    choice = (scores + router_bias.astype(jnp.float32)).reshape(T, N_GROUPS, N_EXPERTS // N_GROUPS)
    group_score = jnp.sum(lax.top_k(choice, 2)[0], axis=-1)
    _, top_groups = lax.top_k(group_score, TOPK_GROUPS)
    group_mask = jnp.any(top_groups[..., None] == jnp.arange(N_GROUPS), axis=1)
    masked = jnp.where(group_mask[:, :, None], choice, -jnp.inf).reshape(T, N_EXPERTS)
    _, top_idx = lax.top_k(masked, TOP_K)
    w = jnp.take_along_axis(scores, top_idx, axis=1)
    w = w / jnp.sum(w, axis=-1, keepdims=True) * ROUTED_SCALE
    return top_idx, w


def routed_experts(xt, top_idx, gate_w, w1, w3, w2):
    T, D = xt.shape
    A = T * TOP_K
    flat_e = top_idx.reshape(A)
    flat_tok = jnp.arange(A, dtype=jnp.int32) // TOP_K
    flat_w = gate_w.reshape(A)
    order = jnp.argsort(flat_e)
    se, stok, sw = flat_e[order], flat_tok[order], flat_w[order]
    counts = jnp.bincount(flat_e, length=N_EXPERTS)
    offsets = jnp.cumsum(counts) - counts
    padded_counts = (counts + MOE_BLOCK - 1) // MOE_BLOCK * MOE_BLOCK
    padded_ends = jnp.cumsum(padded_counts)
    padded_starts = padded_ends - padded_counts
    pos = padded_starts[se] + (jnp.arange(A, dtype=jnp.int32) - offsets[se])
    n_blocks = -(-A // MOE_BLOCK) + N_EXPERTS
    P = n_blocks * MOE_BLOCK
    pad_tok = jnp.full((P,), T, dtype=jnp.int32).at[pos].set(stok)
    pad_w = jnp.zeros((P,), jnp.float32).at[pos].set(sw)
    block_e = jnp.minimum(jnp.searchsorted(padded_ends, jnp.arange(n_blocks) * MOE_BLOCK, side='right'),
                          N_EXPERTS - 1)
    x_pad = jnp.concatenate([xt, jnp.zeros((1, D), xt.dtype)], axis=0)

    def step(acc, inp):
        tok, wt, e = inp
        xb = x_pad[tok]
        hdn = jax.nn.silu(xb @ w1[e]) * (xb @ w3[e])
        yb = (hdn @ w2[e]).astype(jnp.float32) * wt[:, None]
        return acc.at[tok].add(yb), None

    acc0 = jnp.zeros((T + 1, D), jnp.float32)
    acc, _ = lax.scan(step, acc0, (pad_tok.reshape(n_blocks, MOE_BLOCK),
                                   pad_w.reshape(n_blocks, MOE_BLOCK), block_e))
    return acc[:T]


def setup_inputs(seed: int = 0) -> dict:
    key = jax.random.key(seed)
    ks = jax.random.split(key, 24)
    beta = (8.0 * DEPTH) ** -0.25

    def nrm(k, shape, scale):
        return jax.random.normal(k, shape, jnp.float32) * scale

    return {
        "x": nrm(ks[0], (BATCH, SEQ, D_MODEL), 1.0),
        "p": nrm(ks[1], (DEPTH, BATCH, SEQ, PLE_DIM), 1.0),
        "w_in": nrm(ks[2], (DEPTH, D_MODEL, IN_COLS), D_MODEL ** -0.5),
        "pool_w": nrm(ks[3], (DEPTH, N_POOL_GROUPS, POOL_GROUP_DIM, POOL_GROUP_DIM), POOL_GROUP_DIM ** -0.5),
        "pool_scale": 1.0 + nrm(ks[4], (DEPTH, POOL_WIDTH), 0.1),
        "lb_param": nrm(ks[5], (DEPTH + 1, HG_WIDTH), 0.1),
        "hg_norm_w": 1.0 + nrm(ks[6], (DEPTH, HG_WIDTH), 0.1),
        "w_branch_a": nrm(ks[7], (DEPTH, POOL_WIDTH, D_MODEL), POOL_WIDTH ** -0.5),
        "w_branch_b": nrm(ks[8], (DEPTH, HG_WIDTH, D_MODEL), HG_WIDTH ** -0.5),
        "w_out": nrm(ks[9], (DEPTH, D_MODEL, D_MODEL), D_MODEL ** -0.5 * beta),
        "ln1_g": 1.0 + nrm(ks[10], (DEPTH, D_MODEL), 0.1),
        "ln1_b": nrm(ks[11], (DEPTH, D_MODEL), 0.02),
        "router_w": nrm(ks[12], (DEPTH, D_MODEL, N_EXPERTS), D_MODEL ** -0.5),
        "router_bias": nrm(ks[13], (DEPTH, N_EXPERTS), 0.01),
        "exp_w1": nrm(ks[14], (DEPTH, N_EXPERTS, D_MODEL, EXPERT_DIM), D_MODEL ** -0.5),
        "exp_w3": nrm(ks[15], (DEPTH, N_EXPERTS, D_MODEL, EXPERT_DIM), D_MODEL ** -0.5),
        "exp_w2": nrm(ks[16], (DEPTH, N_EXPERTS, EXPERT_DIM, D_MODEL), EXPERT_DIM ** -0.5 * beta),
        "sh_w1": nrm(ks[17], (DEPTH, D_MODEL, SHARED_DIM), D_MODEL ** -0.5),
        "sh_w3": nrm(ks[18], (DEPTH, D_MODEL, SHARED_DIM), D_MODEL ** -0.5),
        "sh_w2": nrm(ks[19], (DEPTH, SHARED_DIM, D_MODEL), SHARED_DIM ** -0.5 * beta),
        "ple_gate_w": nrm(ks[20], (DEPTH, D_MODEL, D_MODEL), D_MODEL ** -0.5),
        "ple_proj_w": nrm(ks[21], (DEPTH, PLE_DIM, D_MODEL), PLE_DIM ** -0.5 * beta),
        "ln2_g": 1.0 + nrm(ks[22], (DEPTH, D_MODEL), 0.1),
        "ln2_b": nrm(ks[23], (DEPTH, D_MODEL), 0.02),
    }


def reference(x, p, w_in, pool_w, pool_scale, lb_param, hg_norm_w, w_branch_a, w_branch_b, w_out,
              ln1_g, ln1_b, router_w, router_bias, exp_w1, exp_w3, exp_w2, sh_w1, sh_w3, sh_w2,
              ple_gate_w, ple_proj_w, ln2_g, ln2_b):
    alpha = (2.0 * DEPTH) ** 0.25
    B, S, D = x.shape
    lb_all = jnp.cumsum(jax.nn.softmax(lb_param.astype(jnp.float32), axis=0), axis=0)
    split_at = list(np.cumsum([POOL_WIDTH, HG_WIDTH, HG_WIDTH, HG_WIDTH, HG_WIDTH, D_MODEL]))
    for i in range(DEPTH):
        proj = x @ w_in[i]
        u_pool, q_raw, f_raw, i_raw, og_raw, ga, gb = jnp.split(proj, split_at, axis=-1)
        y_a = multiscale_pool(u_pool, pool_w[i], pool_scale[i]) @ w_branch_a[i]
        y_b = hgrn2(q_raw, f_raw, i_raw, og_raw, lb_all[i], hg_norm_w[i]) @ w_branch_b[i]
        mixed = jax.nn.sigmoid(ga) * y_a + jax.nn.sigmoid(gb) * y_b
        x = layer_norm(alpha * x + mixed @ w_out[i], ln1_g[i], ln1_b[i])
        xt = x.reshape(B * S, D)
        top_idx, gate_w = route(xt, router_w[i], router_bias[i])
        routed = routed_experts(xt, top_idx, gate_w, exp_w1[i], exp_w3[i], exp_w2[i])
        shared = (jax.nn.silu(xt @ sh_w1[i]) * (xt @ sh_w3[i])) @ sh_w2[i]
        moe_out = (routed + shared.astype(jnp.float32)).astype(x.dtype).reshape(B, S, D)
        ple = jax.nn.sigmoid(x @ ple_gate_w[i]) * (p[i].astype(x.dtype) @ ple_proj_w[i])
        x = layer_norm(alpha * x + moe_out + ple, ln2_g[i], ln2_b[i])
    return x
```

```python
import functools

import numpy as np
import jax
import jax.numpy as jnp
from jax import lax
from jax.experimental import pallas as pl
from jax.experimental.pallas import tpu as pltpu

F32 = jnp.float32
BF16 = jnp.bfloat16
U32 = jnp.uint32
I32 = jnp.int32

V7X_VMEM_BYTES = 64 * 1024 * 1024
VMEM_LIMIT = V7X_VMEM_BYTES - 12 * 1024 * 1024
LANES = 128
SMEM_INDEX_CHUNK = 1024

POOL_WINDOWS = (2, 4, 8, 16)
HG_DIM = 128
N_GROUPS = 8
TOPK_GROUPS = 4
TOP_K = 8
ROUTED_SCALE = 2.5
LN_EPS = 1e-5
RMS_EPS = 1e-6

HG_CHUNK = 64
HG_HEADS_PER_STEP = 2
EXPERT_BLOCK = 256
POOL_HALO = 16


def _tile(n, pref):
    t = min(n, pref)
    while n % t:
        t -= 1
    return t


def _params(*sem):
    return pltpu.CompilerParams(dimension_semantics=sem, vmem_limit_bytes=VMEM_LIMIT)


def _mm_kernel(a_ref, b_ref, o_ref):
    o_ref[...] = jnp.dot(a_ref[...], b_ref[...], preferred_element_type=F32).astype(o_ref.dtype)


def _matmul(a, b, out_dtype, tm, tn):
    M, K = a.shape
    N = b.shape[1]
    tm, tn = _tile(M, tm), _tile(N, tn)
    return pl.pallas_call(
        _mm_kernel,
        grid=(M // tm, N // tn),
        in_specs=[pl.BlockSpec((tm, K), lambda i, j: (i, 0)),
                  pl.BlockSpec((K, tn), lambda i, j: (0, j))],
        out_specs=pl.BlockSpec((tm, tn), lambda i, j: (i, j)),
        out_shape=jax.ShapeDtypeStruct((M, N), out_dtype),
        compiler_params=_params("parallel", "parallel"),
        name="in_proj",
    )(a, b)


def _hgrn_consts():
    C = HG_CHUNK
    t = np.arange(C)[:, None]
    u = np.arange(C)[None, :]
    blocks = [u <= t, u > t]
    masks = []
    h = C // 2
    while h >= 1:
        same = (t // (2 * h)) == (u // (2 * h))
        t2 = (t % (2 * h)) >= h
        u2 = (u % (2 * h)) >= h
        blocks.append(same & ((t2 & u2 & (u <= t)) | (~t2 & ~u2 & (u > t))))
        masks.append(same & t2 & ~u2)
        h //= 2
    masks.append(t == u)
    wall = np.concatenate(blocks, axis=0).astype(np.float32)
    mask = np.stack(masks).astype(np.float32)
    return jnp.asarray(wall, BF16), jnp.asarray(mask, F32)


def _hgrn_kernel(q_ref, f_ref, i_ref, og_ref, lb_ref, nw_ref, wall_ref, mask_ref,
                 o_ref, st_ref, *, heads, nchunk, nlev):
    C = HG_CHUNK
    D = HG_DIM

    @pl.when(pl.program_id(2) == 0)
    def _():
        st_ref[...] = jnp.zeros_like(st_ref)

    nt = (((1,), (1,)), ((), ()))
    tn = (((0,), (0,)), ((), ()))

    def chunk(c, carry):
        r0 = pl.multiple_of(c * C, C)
        rows = pl.ds(r0, C)
        for j in range(heads):
            cs = slice(j * D, (j + 1) * D)
            lb = lb_ref[:, cs]
            fl = f_ref[rows, cs].astype(F32)
            e = jnp.exp(-jnp.abs(fl))
            r = 1.0 / (1.0 + e)
            pos = fl >= 0
            sig_p = jnp.where(pos, r, e * r)
            sig_n = jnp.where(pos, e * r, r)
            g = jnp.log(lb + (1.0 - lb) * sig_p)
            kk = (1.0 - lb) * sig_n
            q = jax.nn.silu(q_ref[rows, cs].astype(F32))
            v = i_ref[rows, cs]

            g_hi = g.astype(BF16)
            g_lo = (g - g_hi.astype(F32)).astype(BF16)
            e2 = jnp.dot(wall_ref[...], jnp.concatenate([g_hi, g_lo], axis=1),
                         preferred_element_type=F32)
            ex = e2[:, :D] + e2[:, D:]
            b = ex[0:C]
            st = st_ref[j]
            o = lax.dot_general((q * jnp.exp(b)).astype(BF16), st.astype(BF16), nt,
                                preferred_element_type=F32)
            qb = q.astype(BF16)
            kb = kk.astype(BF16)
            a = lax.dot_general(qb, kb, nt, preferred_element_type=F32) * mask_ref[nlev]
            for l in range(nlev):
                xl = jnp.exp(ex[(2 + l) * C:(3 + l) * C])
                a = a + lax.dot_general((q * xl).astype(BF16), (kk * xl).astype(BF16), nt,
                                        preferred_element_type=F32) * mask_ref[l]
            o = o + jnp.dot(a.astype(BF16), v, preferred_element_type=F32)
            k_dec = (kk * jnp.exp(ex[C:2 * C])).astype(BF16)
            st_ref[j] = st * jnp.exp(b[C - 1:C, :]) + lax.dot_general(
                v, k_dec, tn, preferred_element_type=F32)

            o = o * lax.rsqrt(jnp.mean(o * o, axis=-1, keepdims=True) + RMS_EPS)
            o = o * nw_ref[:, cs] * jax.nn.silu(og_ref[rows, cs].astype(F32))
            o_ref[rows, cs] = o.astype(o_ref.dtype)
        return carry

    lax.fori_loop(0, nchunk, chunk, 0)


def _hgrn(proj3, lb, norm_w, col_q, width):
    B, S, _ = proj3.shape
    hw = HG_DIM * HG_HEADS_PER_STEP
    ts = _tile(S, 1024)
    wall, mask = _hgrn_consts()
    nlev = mask.shape[0] - 1
    off = [(col_q + k * width) // hw for k in range(4)]

    def seg(k):
        return pl.BlockSpec((None, ts, hw), lambda b, h, s: (b, s, off[k] + h))

    kern = functools.partial(_hgrn_kernel, heads=HG_HEADS_PER_STEP, nchunk=ts // HG_CHUNK, nlev=nlev)
    return pl.pallas_call(
        kern,
        grid=(B, width // hw, S // ts),
        in_specs=[seg(0), seg(1), seg(2), seg(3),
                  pl.BlockSpec((1, hw), lambda b, h, s: (0, h)),
                  pl.BlockSpec((1, hw), lambda b, h, s: (0, h)),
                  pl.BlockSpec(wall.shape, lambda b, h, s: (0, 0)),
                  pl.BlockSpec(mask.shape, lambda b, h, s: (0, 0, 0))],
        out_specs=pl.BlockSpec((None, ts, hw), lambda b, h, s: (b, s, h)),
        out_shape=jax.ShapeDtypeStruct((B, S, width), BF16),
        scratch_shapes=[pltpu.VMEM((HG_HEADS_PER_STEP, HG_DIM, HG_DIM), F32)],
        compiler_params=_params("parallel", "parallel", "arbitrary"),
        name="hgrn2",
    )(proj3, proj3, proj3, proj3, lb, norm_w, wall, mask)


def _mix_kernel(u_ref, o_ref, ga_ref, gb_ref, pw_ref, ps_ref, wa_ref, wb_ref,
                out_ref, pooled_ref, halo_ref, *, tiles_per_seq):
    i = pl.program_id(0)
    tm = u_ref.shape[0]
    gw = pw_ref.shape[1]

    @pl.when(pl.program_id(1) == 0)
    def _():
        @pl.when(i % tiles_per_seq == 0)
        def _():
            halo_ref[...] = jnp.zeros_like(halo_ref)

        u = u_ref[...].astype(F32)
        ext = jnp.concatenate([halo_ref[...], u], axis=0)
        halo_ref[...] = u[tm - POOL_HALO:, :]
        t1 = ((i % tiles_per_seq) * tm + 1 + lax.broadcasted_iota(I32, (tm, 1), 0)).astype(F32)
        for g, w in enumerate(POOL_WINDOWS):
            cs = slice(g * gw, (g + 1) * gw)
            s = ext[:, cs]
            shift = 1
            while shift < w:
                s = s + pltpu.roll(s, shift, 0)
                shift *= 2
            d = s[POOL_HALO:, :] * (1.0 / jnp.minimum(t1, float(w))) - u[:, cs]
            y = jnp.dot(d.astype(BF16), pw_ref[g], preferred_element_type=F32) * ps_ref[:, cs]
            pooled_ref[:, cs] = y.astype(BF16)

    ya = jnp.dot(pooled_ref[...], wa_ref[...], preferred_element_type=F32)
    yb = jnp.dot(o_ref[...], wb_ref[...], preferred_element_type=F32)
    mixed = (jax.nn.sigmoid(ga_ref[...].astype(F32)) * ya
             + jax.nn.sigmoid(gb_ref[...].astype(F32)) * yb)
    out_ref[...] = mixed.astype(out_ref.dtype)


def _mix(proj, o_b, pool_w, pool_scale, wa, wb, seq, col_ga, col_gb):
    T = proj.shape[0]
    pwid = wa.shape[0]
    D = wa.shape[1]
    tm = _tile(seq, 512)
    tn = _tile(D, 1024)
    kern = functools.partial(_mix_kernel, tiles_per_seq=seq // tm)
    return pl.pallas_call(
        kern,
        grid=(T // tm, D // tn),
        in_specs=[pl.BlockSpec((tm, pwid), lambda i, j: (i, 0)),
                  pl.BlockSpec((tm, o_b.shape[1]), lambda i, j: (i, 0)),
                  pl.BlockSpec((tm, tn), lambda i, j: (i, col_ga // tn + j)),
                  pl.BlockSpec((tm, tn), lambda i, j: (i, col_gb // tn + j)),
                  pl.BlockSpec(pool_w.shape, lambda i, j: (0, 0, 0)),
                  pl.BlockSpec((1, pwid), lambda i, j: (0, 0)),
                  pl.BlockSpec((pwid, tn), lambda i, j: (0, j)),
                  pl.BlockSpec((wb.shape[0], tn), lambda i, j: (0, j))],
        out_specs=pl.BlockSpec((tm, tn), lambda i, j: (i, j)),
        out_shape=jax.ShapeDtypeStruct((T, D), BF16),
        scratch_shapes=[pltpu.VMEM((tm, pwid), BF16), pltpu.VMEM((POOL_HALO, pwid), F32)],
        compiler_params=_params("arbitrary", "arbitrary"),
        name="pool_mix",
    )(proj, o_b, proj, proj, pool_w, pool_scale, wa, wb)


def _outproj_kernel(a_ref, w_ref, x_ref, o_ref, *, alpha):
    o_ref[...] = alpha * x_ref[...] + jnp.dot(a_ref[...], w_ref[...], preferred_element_type=F32)


def _outproj(mixed, w_out, x2, alpha):
    T, D = x2.shape
    tm, tn = _tile(T, 512), _tile(D, 1024)
    return pl.pallas_call(
        functools.partial(_outproj_kernel, alpha=alpha),
        grid=(T // tm, D // tn),
        in_specs=[pl.BlockSpec((tm, D), lambda i, j: (i, 0)),
                  pl.BlockSpec((D, tn), lambda i, j: (0, j)),
                  pl.BlockSpec((tm, tn), lambda i, j: (i, j))],
        out_specs=pl.BlockSpec((tm, tn), lambda i, j: (i, j)),
        out_shape=jax.ShapeDtypeStruct((T, D), F32),
        compiler_params=_params("parallel", "parallel"),
        name="out_proj",
    )(mixed, w_out, x2)


def _pack_halves(x):
    n = x.shape[1] // 2
    lo = pltpu.bitcast(x[:, :n].astype(BF16).astype(F32), U32)
    hi = pltpu.bitcast(x[:, n:].astype(BF16).astype(F32), U32)
    return (lo >> 16) | (hi & jnp.uint32(0xFFFF0000))


def _unpack_halves(u):
    lo = pltpu.bitcast(u << 16, F32)
    hi = pltpu.bitcast(u & jnp.uint32(0xFFFF0000), F32)
    return lo, hi


def _ln1_kernel(pre_ref, g_ref, b_ref, rwh_ref, rwl_ref, x1_ref, x1b_ref, x1u_ref, lg_ref):
    x = pre_ref[...]
    mu = jnp.mean(x, axis=-1, keepdims=True)
    xc = x - mu
    var = jnp.mean(xc * xc, axis=-1, keepdims=True)
    y = xc * lax.rsqrt(var + LN_EPS) * g_ref[...] + b_ref[...]
    x1_ref[...] = y
    y_hi = y.astype(BF16)
    x1b_ref[...] = y_hi
    x1u_ref[...] = _pack_halves(y)
    y_lo = (y - y_hi.astype(F32)).astype(BF16)
    nt = (((1,), (1,)), ((), ()))
    lg = lax.dot_general(rwh_ref[...], y_hi, nt, preferred_element_type=F32)
    lg += lax.dot_general(rwh_ref[...], y_lo, nt, preferred_element_type=F32)
    lg += lax.dot_general(rwl_ref[...], y_hi, nt, preferred_element_type=F32)
    lg_ref[...] = lg


def _ln1(pre, g, b, rw_hi_t, rw_lo_t):
    T, D = pre.shape
    E = rw_hi_t.shape[0]
    tm = _tile(T, 256)
    row = lambda i: (i, 0)
    const = lambda i: (0, 0)
    return pl.pallas_call(
        _ln1_kernel,
        grid=(T // tm,),
        in_specs=[pl.BlockSpec((tm, D), row), pl.BlockSpec((1, D), const), pl.BlockSpec((1, D), const),
                  pl.BlockSpec((E, D), const), pl.BlockSpec((E, D), const)],
        out_specs=[pl.BlockSpec((tm, D), row), pl.BlockSpec((tm, D), row),
                   pl.BlockSpec((tm, D // 2), row), pl.BlockSpec((E, tm), lambda i: (0, i))],
        out_shape=[jax.ShapeDtypeStruct((T, D), F32), jax.ShapeDtypeStruct((T, D), BF16),
                   jax.ShapeDtypeStruct((T, D // 2), U32), jax.ShapeDtypeStruct((E, T), F32)],
        compiler_params=_params("parallel"),
        name="ln1_router",
    )(pre, g, b, rw_hi_t, rw_lo_t)


def _route_kernel(lg_ref, bias_ref, tri_ref, idx_ref, w_ref, rank_ref, cnt_ref, carry_ref):
    @pl.when(pl.program_id(0) == 0)
    def _():
        carry_ref[...] = jnp.zeros_like(carry_ref)

    E, tr = lg_ref.shape
    gsz = E // N_GROUPS
    ninf = -jnp.inf
    scores = jax.nn.sigmoid(lg_ref[...])
    choice = scores + bias_ref[...]

    def first_max(x, n):
        io = lax.broadcasted_iota(I32, x.shape, 0)
        m = jnp.max(x, axis=0, keepdims=True)
        return m, jnp.min(jnp.where(x == m, io, n), axis=0, keepdims=True), io

    gs_rows = []
    for g in range(N_GROUPS):
        cg = choice[g * gsz:(g + 1) * gsz, :]
        m1, i1, io = first_max(cg, gsz)
        m2 = jnp.max(jnp.where(io == i1, ninf, cg), axis=0, keepdims=True)
        gs_rows.append(m1 + m2)
    gs = jnp.concatenate(gs_rows, axis=0)
    gsel = jnp.zeros(gs.shape, F32)
    for _ in range(TOPK_GROUPS):
        _, gi, io = first_max(gs, N_GROUPS)
        hit = io == gi
        gsel = jnp.where(hit, 1.0, gsel)
        gs = jnp.where(hit, ninf, gs)
    gsel_e = jnp.concatenate(
        [jnp.broadcast_to(gsel[g:g + 1, :], (gsz, tr)) for g in range(N_GROUPS)], axis=0)
    masked = jnp.where(gsel_e > 0.0, choice, ninf)

    idx_rows, w_rows = [], []
    member = jnp.zeros((E, tr), F32)
    for _ in range(TOP_K):
        _, ii, io = first_max(masked, E)
        hit = io == ii
        idx_rows.append(ii)
        w_rows.append(jnp.sum(jnp.where(hit, scores, 0.0), axis=0, keepdims=True))
        member = jnp.where(hit, 1.0, member)
        masked = jnp.where(hit, ninf, masked)
    wsum = w_rows[0]
    for wk in w_rows[1:]:
        wsum = wsum + wk
    idx_ref[...] = jnp.concatenate(idx_rows, axis=0)
    w_ref[...] = jnp.concatenate([wk / wsum * ROUTED_SCALE for wk in w_rows], axis=0)

    before = jnp.dot(member.astype(BF16), tri_ref[...], preferred_element_type=F32) + carry_ref[...]
    io = lax.broadcasted_iota(I32, (E, tr), 0)
    rank_ref[...] = jnp.concatenate(
        [jnp.sum(jnp.where(io == ii, before, 0.0), axis=0, keepdims=True) for ii in idx_rows],
        axis=0).astype(I32)
    carry_ref[...] += jnp.sum(member, axis=1, keepdims=True)
    cnt_ref[...] = carry_ref[...]


def _route(logits_t, bias):
    E, T = logits_t.shape
    tr = _tile(T, 512)
    tri = jnp.asarray(np.triu(np.ones((tr, tr), np.float32), k=1), BF16)
    col = lambda i: (0, i)
    return pl.pallas_call(
        _route_kernel,
        grid=(T // tr,),
        in_specs=[pl.BlockSpec((E, tr), col), pl.BlockSpec((E, 1), lambda i: (0, 0)),
                  pl.BlockSpec((tr, tr), lambda i: (0, 0))],
        out_specs=[pl.BlockSpec((TOP_K, tr), col), pl.BlockSpec((TOP_K, tr), col),
                   pl.BlockSpec((TOP_K, tr), col), pl.BlockSpec((E, 1), lambda i: (0, 0))],
        out_shape=[jax.ShapeDtypeStruct((TOP_K, T), I32), jax.ShapeDtypeStruct((TOP_K, T), F32),
                   jax.ShapeDtypeStruct((TOP_K, T), I32), jax.ShapeDtypeStruct((E, 1), F32)],
        scratch_shapes=[pltpu.VMEM((E, 1), F32)],
        compiler_params=_params("arbitrary"),
        name="route",
    )(logits_t, bias, tri)


def _gather_kernel(rows_hbm, src_hbm, out_ref, idx_smem, idx_sem, row_sem):
    n = out_ref.shape[0]
    base = pl.multiple_of(pl.program_id(0) * n, SMEM_INDEX_CHUNK)
    idx_cp = pltpu.make_async_copy(rows_hbm.at[pl.ds(base, n)], idx_smem, idx_sem)
    idx_cp.start()
    idx_cp.wait()

    def row_copy(r, src_row):
        return pltpu.make_async_copy(src_hbm.at[pl.ds(src_row, 1)], out_ref.at[pl.ds(r, 1)], row_sem)

    def issue(r, c):
        row_copy(r, idx_smem[r]).start()
        return c

    def drain(r, c):
        row_copy(r, 0).wait()
        return c

    lax.fori_loop(0, n, issue, 0)
    lax.fori_loop(0, n, drain, 0)


def _gather_rows(rows, src):
    P = rows.shape[0]
    W = src.shape[1]
    n = SMEM_INDEX_CHUNK
    return pl.pallas_call(
        _gather_kernel,
        grid=(P // n,),
        in_specs=[pl.BlockSpec(memory_space=pl.ANY), pl.BlockSpec(memory_space=pl.ANY)],
        out_specs=pl.BlockSpec((n, W), lambda i: (i, 0)),
        out_shape=jax.ShapeDtypeStruct((P, W), src.dtype),
        scratch_shapes=[pltpu.SMEM((n,), I32), pltpu.SemaphoreType.DMA(()),
                        pltpu.SemaphoreType.DMA(())],
        compiler_params=_params("arbitrary"),
        name="dispatch_gather",
    )(rows, src)


def _expert_kernel(be_ref, nu_ref, xs_ref, w1_ref, w3_ref, w2_ref, y_ref, w1b, w3b, w2b):
    n = pl.program_id(0)
    used = n < nu_ref[0]

    @pl.when(used)
    def _():
        prev = be_ref[jnp.maximum(n - 1, 0)]

        @pl.when((n == 0) | (be_ref[n] != prev))
        def _():
            w1b[...] = w1_ref[...].astype(BF16)
            w3b[...] = w3_ref[...].astype(BF16)
            w2b[...] = w2_ref[...].astype(BF16)

        half = xs_ref.shape[1]
        lo, hi = _unpack_halves(xs_ref[...])
        lo, hi = lo.astype(BF16), hi.astype(BF16)
        h1 = (jnp.dot(lo, w1b[:half, :], preferred_element_type=F32)
              + jnp.dot(hi, w1b[half:, :], preferred_element_type=F32))
        h3 = (jnp.dot(lo, w3b[:half, :], preferred_element_type=F32)
              + jnp.dot(hi, w3b[half:, :], preferred_element_type=F32))
        hdn = (jax.nn.silu(h1) * h3).astype(BF16)
        y_ref[...] = _pack_halves(jnp.dot(hdn, w2b[...], preferred_element_type=F32))

    @pl.when(jnp.logical_not(used))
    def _():
        y_ref[...] = jnp.zeros_like(y_ref)


def _experts(block_e, n_used, xs, w1, w3, w2):
    P, half = xs.shape
    E, D, H = w1.shape
    nb = P // EXPERT_BLOCK
    grid_spec = pltpu.PrefetchScalarGridSpec(
        num_scalar_prefetch=2,
        grid=(nb,),
        in_specs=[pl.BlockSpec((EXPERT_BLOCK, half), lambda n, be, nu: (n, 0)),
                  pl.BlockSpec((None, D, H), lambda n, be, nu: (be[n], 0, 0)),
                  pl.BlockSpec((None, D, H), lambda n, be, nu: (be[n], 0, 0)),
                  pl.BlockSpec((None, H, D), lambda n, be, nu: (be[n], 0, 0))],
        out_specs=pl.BlockSpec((EXPERT_BLOCK, half), lambda n, be, nu: (n, 0)),
        scratch_shapes=[pltpu.VMEM((D, H), BF16), pltpu.VMEM((D, H), BF16), pltpu.VMEM((H, D), BF16)],
    )
    return pl.pallas_call(
        _expert_kernel,
        grid_spec=grid_spec,
        out_shape=jax.ShapeDtypeStruct((P, half), U32),
        compiler_params=_params("arbitrary"),
        name="experts",
    )(block_e, n_used, xs, w1, w3, w2)


def _addend_kernel(x_ref, p_ref, s1_ref, s3_ref, s2_ref, wg_ref, wp_ref, o_ref, h_ref):
    @pl.when(pl.program_id(1) == 0)
    def _():
        x = x_ref[...]
        h1 = jnp.dot(x, s1_ref[...], preferred_element_type=F32)
        h3 = jnp.dot(x, s3_ref[...], preferred_element_type=F32)
        h_ref[...] = (jax.nn.silu(h1) * h3).astype(BF16)

    shared = jnp.dot(h_ref[...], s2_ref[...], preferred_element_type=F32)
    gate = jax.nn.sigmoid(jnp.dot(x_ref[...], wg_ref[...], preferred_element_type=F32))
    emb = jnp.dot(p_ref[...], wp_ref[...], preferred_element_type=F32)
    o_ref[...] = (shared + gate * emb).astype(o_ref.dtype)


def _addend(x1b, p2, s1, s3, s2, wg, wp):
    T, D = x1b.shape
    H = s1.shape[1]
    PD = p2.shape[1]
    tm, tn = _tile(T, 512), _tile(D, 1024)
    return pl.pallas_call(
        _addend_kernel,
        grid=(T // tm, D // tn),
        in_specs=[pl.BlockSpec((tm, D), lambda i, j: (i, 0)),
                  pl.BlockSpec((tm, PD), lambda i, j: (i, 0)),
                  pl.BlockSpec((D, H), lambda i, j: (0, 0)),
                  pl.BlockSpec((D, H), lambda i, j: (0, 0)),
                  pl.BlockSpec((H, tn), lambda i, j: (0, j)),
                  pl.BlockSpec((D, tn), lambda i, j: (0, j)),
                  pl.BlockSpec((PD, tn), lambda i, j: (0, j))],
        out_specs=pl.BlockSpec((tm, tn), lambda i, j: (i, j)),
        out_shape=jax.ShapeDtypeStruct((T, D), BF16),
        scratch_shapes=[pltpu.VMEM((tm, H), BF16)],
        compiler_params=_params("arbitrary", "arbitrary"),
        name="shared_ple",
    )(x1b, p2, s1, s3, s2, wg, wp)


def _final_kernel(pos_hbm, y_hbm, x1_ref, add_ref, w_ref, g_ref, b_ref, o_ref,
                  buf, idx_smem, idx_sem, row_sem, *, alpha):
    tm = x1_ref.shape[0]
    n = tm * TOP_K
    base = pl.multiple_of(pl.program_id(0) * n, SMEM_INDEX_CHUNK)
    idx_cp = pltpu.make_async_copy(pos_hbm.at[pl.ds(base, n)], idx_smem, idx_sem)
    idx_cp.start()
    idx_cp.wait()

    def row_copy(r, src_row):
        return pltpu.make_async_copy(y_hbm.at[pl.ds(src_row, 1)], buf.at[pl.ds(r, 1)], row_sem)

    def issue(r, c):
        row_copy(r, idx_smem[r]).start()
        return c

    def drain(r, c):
        row_copy(r, 0).wait()
        return c

    lax.fori_loop(0, n, issue, 0)
    lax.fori_loop(0, n, drain, 0)

    half = buf.shape[1]
    w = w_ref[...]
    acc_lo = jnp.zeros((tm, half), F32)
    acc_hi = jnp.zeros((tm, half), F32)
    for k in range(TOP_K):
        lo, hi = _unpack_halves(buf[k * tm:(k + 1) * tm, :])
        wk = w[:, k:k + 1]
        acc_lo += wk * lo
        acc_hi += wk * hi
    pre_lo = alpha * x1_ref[:, :half] + add_ref[:, :half].astype(F32) + acc_lo
    pre_hi = alpha * x1_ref[:, half:] + add_ref[:, half:].astype(F32) + acc_hi
    inv_d = 1.0 / (2 * half)
    mu = (jnp.sum(pre_lo, axis=-1, keepdims=True) + jnp.sum(pre_hi, axis=-1, keepdims=True)) * inv_d
    c_lo = pre_lo - mu
    c_hi = pre_hi - mu
    var = (jnp.sum(c_lo * c_lo, axis=-1, keepdims=True)
           + jnp.sum(c_hi * c_hi, axis=-1, keepdims=True)) * inv_d
    rs = lax.rsqrt(var + LN_EPS)
    o_ref[:, :half] = c_lo * rs * g_ref[:, :half] + b_ref[:, :half]
    o_ref[:, half:] = c_hi * rs * g_ref[:, half:] + b_ref[:, half:]


def _final(pos_flat, y_sorted, x1, addend, w_tok, g, b, alpha):
    T, D = x1.shape
    tm = SMEM_INDEX_CHUNK // TOP_K
    row = lambda i: (i, 0)
    const = lambda i: (0, 0)
    return pl.pallas_call(
        functools.partial(_final_kernel, alpha=alpha),
        grid=(T // tm,),
        in_specs=[pl.BlockSpec(memory_space=pl.ANY), pl.BlockSpec(memory_space=pl.ANY),
                  pl.BlockSpec((tm, D), row), pl.BlockSpec((tm, D), row),
                  pl.BlockSpec((tm, TOP_K), row), pl.BlockSpec((1, D), const), pl.BlockSpec((1, D), const)],
        out_specs=pl.BlockSpec((tm, D), row),
        out_shape=jax.ShapeDtypeStruct((T, D), F32),
        scratch_shapes=[pltpu.VMEM((tm * TOP_K, D // 2), U32), pltpu.SMEM((tm * TOP_K,), I32),
                        pltpu.SemaphoreType.DMA(()), pltpu.SemaphoreType.DMA(())],
        compiler_params=_params("arbitrary"),
        name="combine_ln2",
    )(pos_flat, y_sorted, x1, addend, w_tok, g, b)


def _dispatch_plan(top_idx, rank, counts, T):
    E = counts.shape[0]
    blk = EXPERT_BLOCK
    n_blocks = -(-(T * TOP_K) // blk) + E
    n_blocks = -(-n_blocks * blk // SMEM_INDEX_CHUNK) * SMEM_INDEX_CHUNK // blk
    padded = (counts + blk - 1) // blk * blk
    ends = jnp.cumsum(padded)
    starts = ends - padded
    pos = starts[top_idx] + rank
    tok = jnp.broadcast_to(jnp.arange(T, dtype=I32)[None, :], pos.shape)
    src_rows = jnp.zeros((n_blocks * blk,), I32).at[pos.reshape(-1)].set(tok.reshape(-1))
    block_e = jnp.minimum(
        jnp.searchsorted(ends, jnp.arange(n_blocks, dtype=I32) * blk, side="right"), E - 1).astype(I32)
    n_used = (ends[-1:] // blk).astype(I32)
    return pos, src_rows, block_e, n_used


def kernel(x, p, w_in, pool_w, pool_scale, lb_param, hg_norm_w, w_branch_a, w_branch_b, w_out, ln1_g, ln1_b, router_w, router_bias, exp_w1, exp_w3, exp_w2, sh_w1, sh_w3, sh_w2, ple_gate_w, ple_proj_w, ln2_g, ln2_b):
    B, S, D = x.shape
    T = B * S
    depth = w_in.shape[0]
    alpha = (2.0 * depth) ** 0.25
    pool_width = w_branch_a.shape[1]
    hg_width = w_branch_b.shape[1]
    col_q = pool_width
    col_ga = pool_width + 4 * hg_width
    col_gb = col_ga + D
    lb_all = jnp.cumsum(jax.nn.softmax(lb_param.astype(F32), axis=0), axis=0)

    for i in range(depth):
        x2 = x.reshape(T, D)
        proj = _matmul(x2.astype(BF16), w_in[i].astype(BF16), BF16, 1024, 1024)
        o_b = _hgrn(proj.reshape(B, S, -1), lb_all[i][None, :], hg_norm_w[i][None, :].astype(F32),
                    col_q, hg_width).reshape(T, hg_width)
        mixed = _mix(proj, o_b, pool_w[i].astype(BF16), pool_scale[i][None, :].astype(F32),
                     w_branch_a[i].astype(BF16), w_branch_b[i].astype(BF16), S, col_ga, col_gb)
        pre = _outproj(mixed, w_out[i].astype(BF16), x2, alpha)

        rw_t = router_w[i].astype(F32).T
        rw_hi = rw_t.astype(BF16)
        rw_lo = (rw_t - rw_hi.astype(F32)).astype(BF16)
        x1, x1b, x1u, logits_t = _ln1(pre, ln1_g[i][None, :], ln1_b[i][None, :], rw_hi, rw_lo)

        top_idx, gate_w, rank, counts = _route(logits_t, router_bias[i].astype(F32)[:, None])
        pos, src_rows, block_e, n_used = _dispatch_plan(top_idx, rank, counts[:, 0].astype(I32), T)
        xs = _gather_rows(src_rows, x1u)
        ys = _experts(block_e, n_used, xs, exp_w1[i], exp_w3[i], exp_w2[i])

        addend = _addend(x1b, p[i].reshape(T, -1).astype(BF16), sh_w1[i].astype(BF16),
                         sh_w3[i].astype(BF16), sh_w2[i].astype(BF16),
                         ple_gate_w[i].astype(BF16), ple_proj_w[i].astype(BF16))
        tm = SMEM_INDEX_CHUNK // TOP_K
        pos_flat = pos.reshape(TOP_K, T // tm, tm).transpose(1, 0, 2).reshape(-1)
        out = _final(pos_flat, ys, x1, addend, gate_w.T, ln2_g[i][None, :], ln2_b[i][None, :], alpha)
        x = out.reshape(B, S, D)
    return x
```

```python
import functools

import numpy as np
import jax
import jax.numpy as jnp
from jax import lax
from jax.experimental import pallas as pl
from jax.experimental.pallas import tpu as pltpu

F32 = jnp.float32
BF16 = jnp.bfloat16
U32 = jnp.uint32
I32 = jnp.int32

V7X_VMEM_BYTES = 64 * 1024 * 1024
VMEM_LIMIT = V7X_VMEM_BYTES - 12 * 1024 * 1024
LANES = 128
SMEM_INDEX_CHUNK = 1024

POOL_WINDOWS = (2, 4, 8, 16)
HG_DIM = 128
N_GROUPS = 8
TOPK_GROUPS = 4
TOP_K = 8
ROUTED_SCALE = 2.5
LN_EPS = 1e-5
RMS_EPS = 1e-6

HG_CHUNK = 64
HG_HEADS_PER_STEP = 8
EXPERT_BLOCK = 256
POOL_HALO = 16
DMA_LOOP_UNROLL = 8


def _tile(n, pref):
    t = min(n, pref)
    while n % t:
        t -= 1
    return t


def _params(*sem):
    return pltpu.CompilerParams(dimension_semantics=sem, vmem_limit_bytes=VMEM_LIMIT)


def _mm_kernel(a_ref, b_ref, o_ref):
    o_ref[...] = jnp.dot(a_ref[...], b_ref[...], preferred_element_type=F32).astype(o_ref.dtype)


def _matmul(a, b, out_dtype, tm, tn):
    M, K = a.shape
    N = b.shape[1]
    tm, tn = _tile(M, tm), _tile(N, tn)
    return pl.pallas_call(
        _mm_kernel,
        grid=(M // tm, N // tn),
        in_specs=[pl.BlockSpec((tm, K), lambda i, j: (i, 0)),
                  pl.BlockSpec((K, tn), lambda i, j: (0, j))],
        out_specs=pl.BlockSpec((tm, tn), lambda i, j: (i, j)),
        out_shape=jax.ShapeDtypeStruct((M, N), out_dtype),
        compiler_params=_params("parallel", "parallel"),
        name="in_proj",
    )(a, b)


def _hgrn_consts():
    C = HG_CHUNK
    t = np.arange(C)[:, None]
    u = np.arange(C)[None, :]
    blocks = [u <= t, u > t]
    masks = []
    h = C // 2
    while h >= 1:
        same = (t // (2 * h)) == (u // (2 * h))
        t2 = (t % (2 * h)) >= h
        u2 = (u % (2 * h)) >= h
        blocks.append(same & ((t2 & u2 & (u <= t)) | (~t2 & ~u2 & (u > t))))
        masks.append(same & t2 & ~u2)
        h //= 2
    masks.append(t == u)
    wall = np.concatenate(blocks, axis=0).astype(np.float32)
    mask = np.stack(masks).astype(np.float32)
    return jnp.asarray(wall, BF16), jnp.asarray(mask, F32)


def _hgrn_kernel(q_ref, f_ref, i_ref, og_ref, lb_ref, nw_ref, wall_ref, mask_ref,
                 o_ref, *st_refs, heads, nchunk, nlev):
    C = HG_CHUNK
    D = HG_DIM

    @pl.when(pl.program_id(2) == 0)
    def _():
        for st_ref in st_refs:
            st_ref[...] = jnp.zeros_like(st_ref)

    nt = (((1,), (1,)), ((), ()))
    tn = (((0,), (0,)), ((), ()))

    def chunk(c, carry):
        r0 = pl.multiple_of(c * C, C)
        rows = pl.ds(r0, C)
        hs = range(heads)
        cols = [slice(j * D, (j + 1) * D) for j in hs]
        qs, ks, vs, gcat = [], [], [], []
        for cs in cols:
            lb = lb_ref[:, cs]
            fl = f_ref[rows, cs].astype(F32)
            e = jnp.exp(-jnp.abs(fl))
            r = 1.0 / (1.0 + e)
            pos = fl >= 0
            sig_p = jnp.where(pos, r, e * r)
            sig_n = jnp.where(pos, e * r, r)
            g = jnp.log(lb + (1.0 - lb) * sig_p)
            ks.append((1.0 - lb) * sig_n)
            qs.append(jax.nn.silu(q_ref[rows, cs].astype(F32)))
            vs.append(i_ref[rows, cs])
            g_hi = g.astype(BF16)
            gcat += [g_hi, (g - g_hi.astype(F32)).astype(BF16)]
        e2 = jnp.dot(wall_ref[...], jnp.concatenate(gcat, axis=1), preferred_element_type=F32)
        exs = [e2[:, 2 * j * D:(2 * j + 1) * D] + e2[:, (2 * j + 1) * D:(2 * j + 2) * D] for j in hs]
        sts = [st_refs[j][...] for j in hs]
        o_inter = [lax.dot_general((qs[j] * jnp.exp(exs[j][0:C])).astype(BF16), sts[j].astype(BF16), nt,
                                   preferred_element_type=F32) for j in hs]
        a = [lax.dot_general(qs[j].astype(BF16), ks[j].astype(BF16), nt,
                             preferred_element_type=F32) * mask_ref[nlev] for j in hs]
        for l in range(nlev):
            for j in hs:
                xl = jnp.exp(exs[j][(2 + l) * C:(3 + l) * C])
                a[j] = a[j] + lax.dot_general((qs[j] * xl).astype(BF16), (ks[j] * xl).astype(BF16), nt,
                                              preferred_element_type=F32) * mask_ref[l]
        os_ = [o_inter[j] + jnp.dot(a[j].astype(BF16), vs[j], preferred_element_type=F32) for j in hs]
        for j in hs:
            k_dec = (ks[j] * jnp.exp(exs[j][C:2 * C])).astype(BF16)
            st_refs[j][...] = sts[j] * jnp.exp(exs[j][C - 1:C, :]) + lax.dot_general(
                vs[j], k_dec, tn, preferred_element_type=F32)
        for j, cs in enumerate(cols):
            o = os_[j]
            o = o * lax.rsqrt(jnp.mean(o * o, axis=-1, keepdims=True) + RMS_EPS)
            o = o * nw_ref[:, cs] * jax.nn.silu(og_ref[rows, cs].astype(F32))
            o_ref[rows, cs] = o.astype(o_ref.dtype)
        return carry

    lax.fori_loop(0, nchunk, chunk, 0)


def _hgrn(proj3, lb, norm_w, col_q, width):
    B, S, _ = proj3.shape
    hw = HG_DIM * HG_HEADS_PER_STEP
    ts = _tile(S, 1024)
    wall, mask = _hgrn_consts()
    nlev = mask.shape[0] - 1
    off = [(col_q + k * width) // hw for k in range(4)]

    def seg(k):
        return pl.BlockSpec((None, ts, hw), lambda b, h, s: (b, s, off[k] + h))

    kern = functools.partial(_hgrn_kernel, heads=HG_HEADS_PER_STEP, nchunk=ts // HG_CHUNK, nlev=nlev)
    return pl.pallas_call(
        kern,
        grid=(B, width // hw, S // ts),
        in_specs=[seg(0), seg(1), seg(2), seg(3),
                  pl.BlockSpec((1, hw), lambda b, h, s: (0, h)),
                  pl.BlockSpec((1, hw), lambda b, h, s: (0, h)),
                  pl.BlockSpec(wall.shape, lambda b, h, s: (0, 0)),
                  pl.BlockSpec(mask.shape, lambda b, h, s: (0, 0, 0))],
        out_specs=pl.BlockSpec((None, ts, hw), lambda b, h, s: (b, s, h)),
        out_shape=jax.ShapeDtypeStruct((B, S, width), BF16),
        scratch_shapes=[pltpu.VMEM((HG_DIM, HG_DIM), F32)] * HG_HEADS_PER_STEP,
        compiler_params=_params("parallel", "parallel", "arbitrary"),
        name="hgrn2",
    )(proj3, proj3, proj3, proj3, lb, norm_w, wall, mask)


def _mix_kernel(u_ref, o_ref, ga_ref, gb_ref, pw_ref, ps_ref, wa_ref, wb_ref,
                out_ref, pooled_ref, halo_ref, *, tiles_per_seq):
    i = pl.program_id(0)
    tm = u_ref.shape[0]
    gw = pw_ref.shape[1]

    @pl.when(pl.program_id(1) == 0)
    def _():
        @pl.when(i % tiles_per_seq == 0)
        def _():
            halo_ref[...] = jnp.zeros_like(halo_ref)

        u = u_ref[...].astype(F32)
        ext = jnp.concatenate([halo_ref[...], u], axis=0)
        halo_ref[...] = u[tm - POOL_HALO:, :]
        t1 = ((i % tiles_per_seq) * tm + 1 + lax.broadcasted_iota(I32, (tm, 1), 0)).astype(F32)
        for g, w in enumerate(POOL_WINDOWS):
            cs = slice(g * gw, (g + 1) * gw)
            s = ext[:, cs]
            shift = 1
            while shift < w:
                s = s + pltpu.roll(s, shift, 0)
                shift *= 2
            d = s[POOL_HALO:, :] * (1.0 / jnp.minimum(t1, float(w))) - u[:, cs]
            y = jnp.dot(d.astype(BF16), pw_ref[g], preferred_element_type=F32) * ps_ref[:, cs]
            pooled_ref[:, cs] = y.astype(BF16)

    ya = jnp.dot(pooled_ref[...], wa_ref[...], preferred_element_type=F32)
    yb = jnp.dot(o_ref[...], wb_ref[...], preferred_element_type=F32)
    mixed = (jax.nn.sigmoid(ga_ref[...].astype(F32)) * ya
             + jax.nn.sigmoid(gb_ref[...].astype(F32)) * yb)
    out_ref[...] = mixed.astype(out_ref.dtype)


def _mix(proj, o_b, pool_w, pool_scale, wa, wb, seq, col_ga, col_gb):
    T = proj.shape[0]
    pwid = wa.shape[0]
    D = wa.shape[1]
    tm = _tile(seq, 512)
    tn = _tile(D, 1024)
    kern = functools.partial(_mix_kernel, tiles_per_seq=seq // tm)
    return pl.pallas_call(
        kern,
        grid=(T // tm, D // tn),
        in_specs=[pl.BlockSpec((tm, pwid), lambda i, j: (i, 0)),
                  pl.BlockSpec((tm, o_b.shape[1]), lambda i, j: (i, 0)),
                  pl.BlockSpec((tm, tn), lambda i, j: (i, col_ga // tn + j)),
                  pl.BlockSpec((tm, tn), lambda i, j: (i, col_gb // tn + j)),
                  pl.BlockSpec(pool_w.shape, lambda i, j: (0, 0, 0)),
                  pl.BlockSpec((1, pwid), lambda i, j: (0, 0)),
                  pl.BlockSpec((pwid, tn), lambda i, j: (0, j)),
                  pl.BlockSpec((wb.shape[0], tn), lambda i, j: (0, j))],
        out_specs=pl.BlockSpec((tm, tn), lambda i, j: (i, j)),
        out_shape=jax.ShapeDtypeStruct((T, D), BF16),
        scratch_shapes=[pltpu.VMEM((tm, pwid), BF16), pltpu.VMEM((POOL_HALO, pwid), F32)],
        compiler_params=_params("arbitrary", "arbitrary"),
        name="pool_mix",
    )(proj, o_b, proj, proj, pool_w, pool_scale, wa, wb)


def _outproj_kernel(a_ref, w_ref, x_ref, o_ref, *, alpha):
    o_ref[...] = alpha * x_ref[...] + jnp.dot(a_ref[...], w_ref[...], preferred_element_type=F32)


def _outproj(mixed, w_out, x2, alpha):
    T, D = x2.shape
    tm, tn = _tile(T, 512), _tile(D, 1024)
    return pl.pallas_call(
        functools.partial(_outproj_kernel, alpha=alpha),
        grid=(T // tm, D // tn),
        in_specs=[pl.BlockSpec((tm, D), lambda i, j: (i, 0)),
                  pl.BlockSpec((D, tn), lambda i, j: (0, j)),
                  pl.BlockSpec((tm, tn), lambda i, j: (i, j))],
        out_specs=pl.BlockSpec((tm, tn), lambda i, j: (i, j)),
        out_shape=jax.ShapeDtypeStruct((T, D), F32),
        compiler_params=_params("parallel", "parallel"),
        name="out_proj",
    )(mixed, w_out, x2)


def _pack_halves(x):
    n = x.shape[1] // 2
    lo = pltpu.bitcast(x[:, :n].astype(BF16).astype(F32), U32)
    hi = pltpu.bitcast(x[:, n:].astype(BF16).astype(F32), U32)
    return (lo >> 16) | (hi & jnp.uint32(0xFFFF0000))


def _unpack_halves(u):
    lo = pltpu.bitcast(u << 16, F32)
    hi = pltpu.bitcast(u & jnp.uint32(0xFFFF0000), F32)
    return lo, hi


SUBLANES = 8


def _exchange(a, inner):
    n = a.shape[0]
    s_idx = lax.broadcasted_iota(I32, (1, 1, SUBLANES, LANES), 2)
    d = SUBLANES // 2
    while d >= 1:
        span = d * inner
        a4 = a.reshape(n // (2 * span), 2, span, SUBLANES, LANES)
        lo, hi = a4[:, 0], a4[:, 1]
        keep = (s_idx & d) == 0
        new_lo = jnp.where(keep, lo, pltpu.roll(hi, d, 2))
        new_hi = jnp.where(keep, pltpu.roll(lo, SUBLANES - d, 2), hi)
        a = jnp.stack([new_lo, new_hi], axis=1).reshape(n, SUBLANES, LANES)
        d //= 2
    return a


def _rows_to_tiles(x3):
    r, c, _ = x3.shape
    inner = c // SUBLANES
    a = _exchange(x3.reshape(r * inner, SUBLANES, LANES), inner)
    a = a.reshape(r // SUBLANES, SUBLANES, inner, SUBLANES, LANES)
    return jnp.concatenate([a[:, ci % SUBLANES, ci // SUBLANES].reshape(r, LANES) for ci in range(c)],
                           axis=1)


def _tiles_to_rows(x, c):
    r = x.shape[0]
    inner = c // SUBLANES
    g = r // SUBLANES
    cols = [x[:, ci * LANES:(ci + 1) * LANES].reshape(g, 1, 1, SUBLANES, LANES) for ci in range(c)]
    a = jnp.concatenate(
        [jnp.concatenate([cols[hi * SUBLANES + lo] for hi in range(inner)], axis=2)
         for lo in range(SUBLANES)], axis=1)
    a = _exchange(a.reshape(r * inner, SUBLANES, LANES), inner)
    return a.reshape(r, c, LANES)


def _ln1_kernel(pre_ref, g_ref, b_ref, rwh_ref, rwl_ref, x1_ref, x1b_ref, x1u_ref, lg_ref):
    x = pre_ref[...]
    mu = jnp.mean(x, axis=-1, keepdims=True)
    xc = x - mu
    var = jnp.mean(xc * xc, axis=-1, keepdims=True)
    y = xc * lax.rsqrt(var + LN_EPS) * g_ref[...] + b_ref[...]
    x1_ref[...] = y
    y_hi = y.astype(BF16)
    x1b_ref[...] = y_hi
    x1u_ref[...] = _tiles_to_rows(_pack_halves(y), x1u_ref.shape[1])
    y_lo = (y - y_hi.astype(F32)).astype(BF16)
    nt = (((1,), (1,)), ((), ()))
    lg = lax.dot_general(rwh_ref[...], y_hi, nt, preferred_element_type=F32)
    lg += lax.dot_general(rwh_ref[...], y_lo, nt, preferred_element_type=F32)
    lg += lax.dot_general(rwl_ref[...], y_hi, nt, preferred_element_type=F32)
    lg_ref[...] = lg


def _ln1(pre, g, b, rw_hi_t, rw_lo_t):
    T, D = pre.shape
    E = rw_hi_t.shape[0]
    tm = _tile(T, 256)
    row = lambda i: (i, 0)
    const = lambda i: (0, 0)
    return pl.pallas_call(
        _ln1_kernel,
        grid=(T // tm,),
        in_specs=[pl.BlockSpec((tm, D), row), pl.BlockSpec((1, D), const), pl.BlockSpec((1, D), const),
                  pl.BlockSpec((E, D), const), pl.BlockSpec((E, D), const)],
        out_specs=[pl.BlockSpec((tm, D), row), pl.BlockSpec((tm, D), row),
                   pl.BlockSpec((tm, D // 2 // LANES, LANES), lambda i: (i, 0, 0)),
                   pl.BlockSpec((E, tm), lambda i: (0, i))],
        out_shape=[jax.ShapeDtypeStruct((T, D), F32), jax.ShapeDtypeStruct((T, D), BF16),
                   jax.ShapeDtypeStruct((T, D // 2 // LANES, LANES), U32),
                   jax.ShapeDtypeStruct((E, T), F32)],
        compiler_params=_params("parallel"),
        name="ln1_router",
    )(pre, g, b, rw_hi_t, rw_lo_t)


def _route_kernel(lg_ref, bias_ref, tri_ref, idx_ref, w_ref, rank_ref, cnt_ref, carry_ref):
    @pl.when(pl.program_id(0) == 0)
    def _():
        carry_ref[...] = jnp.zeros_like(carry_ref)

    E, tr = lg_ref.shape
    gsz = E // N_GROUPS
    ninf = -jnp.inf
    scores = jax.nn.sigmoid(lg_ref[...])
    choice = scores + bias_ref[...]

    def first_max(x, n):
        io = lax.broadcasted_iota(I32, x.shape, 0)
        m = jnp.max(x, axis=0, keepdims=True)
        return m, jnp.min(jnp.where(x == m, io, n), axis=0, keepdims=True), io

    gs_rows = []
    for g in range(N_GROUPS):
        cg = choice[g * gsz:(g + 1) * gsz, :]
        m1, i1, io = first_max(cg, gsz)
        m2 = jnp.max(jnp.where(io == i1, ninf, cg), axis=0, keepdims=True)
        gs_rows.append(m1 + m2)
    gs = jnp.concatenate(gs_rows, axis=0)
    gsel = jnp.zeros(gs.shape, F32)
    for _ in range(TOPK_GROUPS):
        _, gi, io = first_max(gs, N_GROUPS)
        hit = io == gi
        gsel = jnp.where(hit, 1.0, gsel)
        gs = jnp.where(hit, ninf, gs)
    gsel_e = jnp.concatenate(
        [jnp.broadcast_to(gsel[g:g + 1, :], (gsz, tr)) for g in range(N_GROUPS)], axis=0)
    masked = jnp.where(gsel_e > 0.0, choice, ninf)

    idx_rows, w_rows = [], []
    member = jnp.zeros((E, tr), F32)
    for _ in range(TOP_K):
        _, ii, io = first_max(masked, E)
        hit = io == ii
        idx_rows.append(ii)
        w_rows.append(jnp.sum(jnp.where(hit, scores, 0.0), axis=0, keepdims=True))
        member = jnp.where(hit, 1.0, member)
        masked = jnp.where(hit, ninf, masked)
    wsum = w_rows[0]
    for wk in w_rows[1:]:
        wsum = wsum + wk
    idx_ref[...] = jnp.concatenate(idx_rows, axis=0)
    w_ref[...] = jnp.concatenate([wk / wsum * ROUTED_SCALE for wk in w_rows], axis=0)

    before = jnp.dot(member.astype(BF16), tri_ref[...], preferred_element_type=F32) + carry_ref[...]
    io = lax.broadcasted_iota(I32, (E, tr), 0)
    rank_ref[...] = jnp.concatenate(
        [jnp.sum(jnp.where(io == ii, before, 0.0), axis=0, keepdims=True) for ii in idx_rows],
        axis=0).astype(I32)
    carry_ref[...] += jnp.sum(member, axis=1, keepdims=True)
    cnt_ref[...] = carry_ref[...]


def _route(logits_t, bias):
    E, T = logits_t.shape
    tr = _tile(T, 512)
    tri = jnp.asarray(np.triu(np.ones((tr, tr), np.float32), k=1), BF16)
    col = lambda i: (0, i)
    return pl.pallas_call(
        _route_kernel,
        grid=(T // tr,),
        in_specs=[pl.BlockSpec((E, tr), col), pl.BlockSpec((E, 1), lambda i: (0, 0)),
                  pl.BlockSpec((tr, tr), lambda i: (0, 0))],
        out_specs=[pl.BlockSpec((TOP_K, tr), col), pl.BlockSpec((TOP_K, tr), col),
                   pl.BlockSpec((TOP_K, tr), col), pl.BlockSpec((E, 1), lambda i: (0, 0))],
        out_shape=[jax.ShapeDtypeStruct((TOP_K, T), I32), jax.ShapeDtypeStruct((TOP_K, T), F32),
                   jax.ShapeDtypeStruct((TOP_K, T), I32), jax.ShapeDtypeStruct((E, 1), F32)],
        scratch_shapes=[pltpu.VMEM((E, 1), F32)],
        compiler_params=_params("arbitrary"),
        name="route",
    )(logits_t, bias, tri)


DISPATCH_TOKENS = SMEM_INDEX_CHUNK // TOP_K
PAD_SIZES = tuple(1 << s for s in reversed(range((EXPERT_BLOCK - 1).bit_length())))
assert PAD_SIZES[0] <= DISPATCH_TOKENS


def _dispatch_kernel(pad_start_ref, pad_len_ref, pos_hbm, x_ref, xs_hbm,
                     idx_smem, idx_sem, row_sem, pad_sem):
    i = pl.program_id(0)
    n = idx_smem.shape[0]
    base = pl.multiple_of(i * n, SMEM_INDEX_CHUNK)
    idx_cp = pltpu.make_async_copy(pos_hbm.at[pl.ds(base, n)], idx_smem, idx_sem)
    idx_cp.start()
    idx_cp.wait()

    def row_copy(t, dst_row):
        return pltpu.make_async_copy(x_ref.at[t], xs_hbm.at[dst_row], row_sem)

    def issue(t, c):
        for k in range(TOP_K):
            row_copy(t, idx_smem[t * TOP_K + k]).start()
        return c

    def drain(r, c):
        row_copy(0, 0).wait()
        return c

    lax.fori_loop(0, n // TOP_K, issue, 0)

    @pl.when(i == 0)
    def _():
        def pad_pass(act):
            def per_expert(e, c):
                ln = pad_len_ref[e]
                off = pad_start_ref[e]
                for sz in PAD_SIZES:
                    @pl.when((ln & sz) != 0)
                    def _():
                        dst = off + (ln & ~(2 * sz - 1))
                        act(pltpu.make_async_copy(x_ref.at[pl.ds(0, sz)], xs_hbm.at[pl.ds(dst, sz)],
                                                  pad_sem))
                return c
            lax.fori_loop(0, pad_len_ref.shape[0], per_expert, 0)

            run = PAD_SIZES[0]
            first = pad_start_ref[pad_len_ref.shape[0]]

            def per_run(q, c):
                act(pltpu.make_async_copy(x_ref.at[pl.ds(0, run)],
                                          xs_hbm.at[pl.ds(first + q * run, run)], pad_sem))
                return c
            lax.fori_loop(0, (xs_hbm.shape[0] - first) // run, per_run, 0)

        pad_pass(lambda cp: cp.start())
        pad_pass(lambda cp: cp.wait())

    lax.fori_loop(0, n, drain, 0, unroll=DMA_LOOP_UNROLL)


def _dispatch(pad_start, pad_len, pos_tok_major, x1u, n_rows):
    T, S, L = x1u.shape
    tt = DISPATCH_TOKENS
    grid_spec = pltpu.PrefetchScalarGridSpec(
        num_scalar_prefetch=2,
        grid=(T // tt,),
        in_specs=[pl.BlockSpec(memory_space=pl.ANY),
                  pl.BlockSpec((tt, S, L), lambda i, ps, pn: (i, 0, 0))],
        out_specs=pl.BlockSpec(memory_space=pl.ANY),
        scratch_shapes=[pltpu.SMEM((tt * TOP_K,), I32), pltpu.SemaphoreType.DMA(()),
                        pltpu.SemaphoreType.DMA(()), pltpu.SemaphoreType.DMA(())],
    )
    return pl.pallas_call(
        _dispatch_kernel,
        grid_spec=grid_spec,
        out_shape=jax.ShapeDtypeStruct((n_rows, S, L), x1u.dtype),
        compiler_params=_params("arbitrary"),
        name="dispatch",
    )(pad_start, pad_len, pos_tok_major, x1u)


def _expert_kernel(be_ref, nu_ref, xs_ref, w1_ref, w3_ref, w2_ref, y_ref, w1b, w3b, w2b):
    n = pl.program_id(0)
    used = n < nu_ref[0]

    @pl.when(jnp.logical_not(used))
    def _():
        y_ref[...] = jnp.zeros_like(y_ref)

    @pl.when(used)
    def _():
        prev = be_ref[jnp.maximum(n - 1, 0)]

        @pl.when((n == 0) | (be_ref[n] != prev))
        def _():
            w1b[...] = w1_ref[...].astype(BF16)
            w3b[...] = w3_ref[...].astype(BF16)
            w2b[...] = w2_ref[...].astype(BF16)

        lo, hi = _unpack_halves(_rows_to_tiles(xs_ref[...]))
        half = lo.shape[1]
        lo, hi = lo.astype(BF16), hi.astype(BF16)
        h1 = (jnp.dot(lo, w1b[:half, :], preferred_element_type=F32)
              + jnp.dot(hi, w1b[half:, :], preferred_element_type=F32))
        h3 = (jnp.dot(lo, w3b[:half, :], preferred_element_type=F32)
              + jnp.dot(hi, w3b[half:, :], preferred_element_type=F32))
        hdn = (jax.nn.silu(h1) * h3).astype(BF16)
        y = _pack_halves(jnp.dot(hdn, w2b[...], preferred_element_type=F32))
        y_ref[...] = _tiles_to_rows(y, y_ref.shape[1])


def _experts(block_e, n_used, xs, w1, w3, w2):
    P, S, L = xs.shape
    E, D, H = w1.shape
    nb = P // EXPERT_BLOCK
    rows = lambda n, be, nu: (jnp.minimum(n, nu[0] - 1), 0, 0)
    grid_spec = pltpu.PrefetchScalarGridSpec(
        num_scalar_prefetch=2,
        grid=(nb,),
        in_specs=[pl.BlockSpec((EXPERT_BLOCK, S, L), rows),
                  pl.BlockSpec((None, D, H), lambda n, be, nu: (be[n], 0, 0)),
                  pl.BlockSpec((None, D, H), lambda n, be, nu: (be[n], 0, 0)),
                  pl.BlockSpec((None, H, D), lambda n, be, nu: (be[n], 0, 0))],
        out_specs=pl.BlockSpec((EXPERT_BLOCK, S, L), lambda n, be, nu: (n, 0, 0)),
        scratch_shapes=[pltpu.VMEM((D, H), BF16), pltpu.VMEM((D, H), BF16), pltpu.VMEM((H, D), BF16)],
    )
    return pl.pallas_call(
        _expert_kernel,
        grid_spec=grid_spec,
        out_shape=jax.ShapeDtypeStruct((P, S, L), U32),
        compiler_params=_params("arbitrary"),
        name="experts",
    )(block_e, n_used, xs, w1, w3, w2)


def _addend_kernel(x_ref, p_ref, s1_ref, s3_ref, s2_ref, wg_ref, wp_ref, o_ref, h_ref):
    @pl.when(pl.program_id(1) == 0)
    def _():
        x = x_ref[...]
        h1 = jnp.dot(x, s1_ref[...], preferred_element_type=F32)
        h3 = jnp.dot(x, s3_ref[...], preferred_element_type=F32)
        h_ref[...] = (jax.nn.silu(h1) * h3).astype(BF16)

    shared = jnp.dot(h_ref[...], s2_ref[...], preferred_element_type=F32)
    gate = jax.nn.sigmoid(jnp.dot(x_ref[...], wg_ref[...], preferred_element_type=F32))
    emb = jnp.dot(p_ref[...], wp_ref[...], preferred_element_type=F32)
    o_ref[...] = (shared + gate * emb).astype(o_ref.dtype)


def _addend(x1b, p2, s1, s3, s2, wg, wp):
    T, D = x1b.shape
    H = s1.shape[1]
    PD = p2.shape[1]
    tm, tn = _tile(T, 512), _tile(D, 1024)
    return pl.pallas_call(
        _addend_kernel,
        grid=(T // tm, D // tn),
        in_specs=[pl.BlockSpec((tm, D), lambda i, j: (i, 0)),
                  pl.BlockSpec((tm, PD), lambda i, j: (i, 0)),
                  pl.BlockSpec((D, H), lambda i, j: (0, 0)),
                  pl.BlockSpec((D, H), lambda i, j: (0, 0)),
                  pl.BlockSpec((H, tn), lambda i, j: (0, j)),
                  pl.BlockSpec((D, tn), lambda i, j: (0, j)),
                  pl.BlockSpec((PD, tn), lambda i, j: (0, j))],
        out_specs=pl.BlockSpec((tm, tn), lambda i, j: (i, j)),
        out_shape=jax.ShapeDtypeStruct((T, D), BF16),
        scratch_shapes=[pltpu.VMEM((tm, H), BF16)],
        compiler_params=_params("arbitrary", "arbitrary"),
        name="shared_ple",
    )(x1b, p2, s1, s3, s2, wg, wp)


def _final_kernel(pos_hbm, y_hbm, x1_ref, add_ref, w_ref, g_ref, b_ref, o_ref,
                  buf, idx_smem, idx_sem, row_sems, *, alpha):
    i = pl.program_id(0)
    tm = x1_ref.shape[0]
    n = tm * TOP_K

    def row_copy(slot, r, src_row):
        return pltpu.make_async_copy(y_hbm.at[src_row], buf.at[slot, r], row_sems.at[slot])

    def fetch(step, slot):
        base = pl.multiple_of(step * n, SMEM_INDEX_CHUNK)
        idx_cp = pltpu.make_async_copy(pos_hbm.at[pl.ds(base, n)], idx_smem, idx_sem)
        idx_cp.start()
        idx_cp.wait()

        def issue(r, c):
            row_copy(slot, r, idx_smem[r]).start()
            return c

        lax.fori_loop(0, n, issue, 0, unroll=DMA_LOOP_UNROLL)

    @pl.when(i == 0)
    def _():
        fetch(0, 0)

    @pl.when(i + 1 < pl.num_programs(0))
    def _():
        fetch(i + 1, (i + 1) % 2)

    slot = i % 2

    def drain(r, c):
        row_copy(slot, 0, 0).wait()
        return c

    lax.fori_loop(0, n, drain, 0, unroll=DMA_LOOP_UNROLL)

    half = buf.shape[2] * LANES
    w = w_ref[...]
    acc_lo = jnp.zeros((tm, half), F32)
    acc_hi = jnp.zeros((tm, half), F32)
    for k in range(TOP_K):
        lo, hi = _unpack_halves(_rows_to_tiles(buf[slot, k * tm:(k + 1) * tm]))
        wk = w[:, k:k + 1]
        acc_lo += wk * lo
        acc_hi += wk * hi
    pre = alpha * x1_ref[...] + add_ref[...].astype(F32) + jnp.concatenate([acc_lo, acc_hi], axis=1)
    mu = jnp.mean(pre, axis=-1, keepdims=True)
    xc = pre - mu
    var = jnp.mean(xc * xc, axis=-1, keepdims=True)
    o_ref[...] = xc * lax.rsqrt(var + LN_EPS) * g_ref[...] + b_ref[...]


def _final(pos_flat, y_sorted, x1, addend, w_tok, g, b, alpha):
    T, D = x1.shape
    _, S, L = y_sorted.shape
    tm = SMEM_INDEX_CHUNK // TOP_K
    row = lambda i: (i, 0)
    const = lambda i: (0, 0)
    return pl.pallas_call(
        functools.partial(_final_kernel, alpha=alpha),
        grid=(T // tm,),
        in_specs=[pl.BlockSpec(memory_space=pl.ANY), pl.BlockSpec(memory_space=pl.ANY),
                  pl.BlockSpec((tm, D), row), pl.BlockSpec((tm, D), row),
                  pl.BlockSpec((tm, TOP_K), row), pl.BlockSpec((1, D), const), pl.BlockSpec((1, D), const)],
        out_specs=pl.BlockSpec((tm, D), row),
        out_shape=jax.ShapeDtypeStruct((T, D), F32),
        scratch_shapes=[pltpu.VMEM((2, tm * TOP_K, S, L), U32), pltpu.SMEM((tm * TOP_K,), I32),
                        pltpu.SemaphoreType.DMA(()), pltpu.SemaphoreType.DMA((2,))],
        compiler_params=_params("arbitrary"),
        name="combine_ln2",
    )(pos_flat, y_sorted, x1, addend, w_tok, g, b)


def _dispatch_plan(top_idx, rank, counts, T):
    E = counts.shape[0]
    blk = EXPERT_BLOCK
    n_blocks = -(-(T * TOP_K) // blk) + E
    padded = (counts + blk - 1) // blk * blk
    ends = jnp.cumsum(padded)
    starts = ends - padded
    onehot = top_idx[None, :, :] == jnp.arange(E, dtype=I32)[:, None, None]
    pos = jnp.sum(jnp.where(onehot, starts[:, None, None], 0), axis=0) + rank
    first_row = jnp.arange(n_blocks, dtype=I32) * blk
    block_e = jnp.minimum(jnp.sum(ends[None, :] <= first_row[:, None], axis=1), E - 1).astype(I32)
    n_used = (ends[-1:] // blk).astype(I32)
    pad_start = jnp.concatenate([starts + counts, ends[-1:]])
    return pos, pad_start, padded - counts, block_e, n_used, n_blocks * blk


def kernel(x, p, w_in, pool_w, pool_scale, lb_param, hg_norm_w, w_branch_a, w_branch_b, w_out, ln1_g, ln1_b, router_w, router_bias, exp_w1, exp_w3, exp_w2, sh_w1, sh_w3, sh_w2, ple_gate_w, ple_proj_w, ln2_g, ln2_b):
    B, S, D = x.shape
    T = B * S
    depth = w_in.shape[0]
    alpha = (2.0 * depth) ** 0.25
    pool_width = w_branch_a.shape[1]
    hg_width = w_branch_b.shape[1]
    col_q = pool_width
    col_ga = pool_width + 4 * hg_width
    col_gb = col_ga + D
    lb_all = jnp.cumsum(jax.nn.softmax(lb_param.astype(F32), axis=0), axis=0)

    for i in range(depth):
        x2 = x.reshape(T, D)
        proj = _matmul(x2.astype(BF16), w_in[i].astype(BF16), BF16, 1024, 1024)
        o_b = _hgrn(proj.reshape(B, S, -1), lb_all[i][None, :], hg_norm_w[i][None, :].astype(F32),
                    col_q, hg_width).reshape(T, hg_width)
        mixed = _mix(proj, o_b, pool_w[i].astype(BF16), pool_scale[i][None, :].astype(F32),
                     w_branch_a[i].astype(BF16), w_branch_b[i].astype(BF16), S, col_ga, col_gb)
        pre = _outproj(mixed, w_out[i].astype(BF16), x2, alpha)

        rw_t = router_w[i].astype(F32).T
        rw_hi = rw_t.astype(BF16)
        rw_lo = (rw_t - rw_hi.astype(F32)).astype(BF16)
        x1, x1b, x1u, logits_t = _ln1(pre, ln1_g[i][None, :], ln1_b[i][None, :], rw_hi, rw_lo)

        top_idx, gate_w, rank, counts = _route(logits_t, router_bias[i].astype(F32)[:, None])
        pos, pad_start, pad_len, block_e, n_used, n_rows = _dispatch_plan(
            top_idx, rank, counts[:, 0].astype(I32), T)
        xs = _dispatch(pad_start, pad_len, pos.T.reshape(-1), x1u, n_rows)
        ys = _experts(block_e, n_used, xs, exp_w1[i], exp_w3[i], exp_w2[i])

        addend = _addend(x1b, p[i].reshape(T, -1).astype(BF16), sh_w1[i].astype(BF16),
                         sh_w3[i].astype(BF16), sh_w2[i].astype(BF16),
                         ple_gate_w[i].astype(BF16), ple_proj_w[i].astype(BF16))
        tm = SMEM_INDEX_CHUNK // TOP_K
        pos_flat = pos.reshape(TOP_K, T // tm, tm).transpose(1, 0, 2).reshape(-1)
        out = _final(pos_flat, ys, x1, addend, gate_w.T, ln2_g[i][None, :], ln2_b[i][None, :], alpha)
        x = out.reshape(B, S, D)
    return x
```

```python
import functools

import numpy as np
import jax
import jax.numpy as jnp
from jax import lax
from jax.experimental import pallas as pl
from jax.experimental.pallas import tpu as pltpu

F32 = jnp.float32
BF16 = jnp.bfloat16
U32 = jnp.uint32
I32 = jnp.int32

V7X_VMEM_BYTES = 64 * 1024 * 1024
VMEM_LIMIT = V7X_VMEM_BYTES - 12 * 1024 * 1024
EXPERT_VMEM_LIMIT = V7X_VMEM_BYTES - 6 * 1024 * 1024
LANES = 128
SMEM_INDEX_CHUNK = 1024

POOL_WINDOWS = (2, 4, 8, 16)
HG_DIM = 128
N_GROUPS = 8
TOPK_GROUPS = 4
TOP_K = 8
ROUTED_SCALE = 2.5
LN_EPS = 1e-5
RMS_EPS = 1e-6

HG_CHUNK = 64
HG_HEADS_PER_STEP = 8
EXPERT_BLOCK = 256
POOL_HALO = 16
DMA_LOOP_UNROLL = 8


def _tile(n, pref):
    t = min(n, pref)
    while n % t:
        t -= 1
    return t


def _params(*sem):
    return pltpu.CompilerParams(dimension_semantics=sem, vmem_limit_bytes=VMEM_LIMIT)


def _mm_kernel(a_ref, w_ref, o_ref, wb_ref):
    @pl.when(pl.program_id(1) == 0)
    def _():
        wb_ref[...] = w_ref[...].astype(BF16)

    o_ref[...] = jnp.dot(a_ref[...], wb_ref[...], preferred_element_type=F32).astype(o_ref.dtype)


def _matmul(a, w, out_dtype, tm, tn):
    M, K = a.shape
    N = w.shape[1]
    tm, tn = _tile(M, tm), _tile(N, tn)
    return pl.pallas_call(
        _mm_kernel,
        grid=(N // tn, M // tm),
        in_specs=[pl.BlockSpec((tm, K), lambda j, i: (i, 0)),
                  pl.BlockSpec((K, tn), lambda j, i: (0, j))],
        out_specs=pl.BlockSpec((tm, tn), lambda j, i: (i, j)),
        out_shape=jax.ShapeDtypeStruct((M, N), out_dtype),
        scratch_shapes=[pltpu.VMEM((K, tn), BF16)],
        compiler_params=_params("arbitrary", "arbitrary"),
        name="in_proj",
    )(a, w)


def _hgrn_consts():
    C = HG_CHUNK
    t = np.arange(C)[:, None]
    u = np.arange(C)[None, :]
    blocks = [u <= t, u > t]
    masks = []
    h = C // 2
    while h >= 1:
        same = (t // (2 * h)) == (u // (2 * h))
        t2 = (t % (2 * h)) >= h
        u2 = (u % (2 * h)) >= h
        blocks.append(same & ((t2 & u2 & (u <= t)) | (~t2 & ~u2 & (u > t))))
        masks.append(same & t2 & ~u2)
        h //= 2
    masks.append(t == u)
    wall = np.concatenate(blocks, axis=0).astype(np.float32)
    mask = np.stack(masks).astype(np.float32)
    return jnp.asarray(wall, BF16), jnp.asarray(mask, F32)


def _hgrn_kernel(q_ref, f_ref, i_ref, og_ref, lb_ref, nw_ref, wall_ref, mask_ref,
                 o_ref, *st_refs, heads, nchunk, nlev):
    C = HG_CHUNK
    D = HG_DIM

    @pl.when(pl.program_id(2) == 0)
    def _():
        for st_ref in st_refs:
            st_ref[...] = jnp.zeros_like(st_ref)

    nt = (((1,), (1,)), ((), ()))
    tn = (((0,), (0,)), ((), ()))

    def chunk(c, carry):
        r0 = pl.multiple_of(c * C, C)
        rows = pl.ds(r0, C)
        hs = range(heads)
        cols = [slice(j * D, (j + 1) * D) for j in hs]
        qs, ks, vs, gcat = [], [], [], []
        for cs in cols:
            lb = lb_ref[:, cs]
            fl = f_ref[rows, cs].astype(F32)
            e = jnp.exp(-jnp.abs(fl))
            r = 1.0 / (1.0 + e)
            pos = fl >= 0
            sig_p = jnp.where(pos, r, e * r)
            sig_n = jnp.where(pos, e * r, r)
            g = jnp.log(lb + (1.0 - lb) * sig_p)
            ks.append((1.0 - lb) * sig_n)
            qs.append(jax.nn.silu(q_ref[rows, cs].astype(F32)))
            vs.append(i_ref[rows, cs])
            g_hi = g.astype(BF16)
            gcat += [g_hi, (g - g_hi.astype(F32)).astype(BF16)]
        e2 = jnp.dot(wall_ref[...], jnp.concatenate(gcat, axis=1), preferred_element_type=F32)
        exs = [e2[:, 2 * j * D:(2 * j + 1) * D] + e2[:, (2 * j + 1) * D:(2 * j + 2) * D] for j in hs]
        sts = [st_refs[j][...] for j in hs]
        o_inter = [lax.dot_general((qs[j] * jnp.exp(exs[j][0:C])).astype(BF16), sts[j].astype(BF16), nt,
                                   preferred_element_type=F32) for j in hs]
        a = [lax.dot_general(qs[j].astype(BF16), ks[j].astype(BF16), nt,
                             preferred_element_type=F32) * mask_ref[nlev] for j in hs]
        for l in range(nlev):
            for j in hs:
                xl = jnp.exp(exs[j][(2 + l) * C:(3 + l) * C])
                a[j] = a[j] + lax.dot_general((qs[j] * xl).astype(BF16), (ks[j] * xl).astype(BF16), nt,
                                              preferred_element_type=F32) * mask_ref[l]
        os_ = [o_inter[j] + jnp.dot(a[j].astype(BF16), vs[j], preferred_element_type=F32) for j in hs]
        for j in hs:
            k_dec = (ks[j] * jnp.exp(exs[j][C:2 * C])).astype(BF16)
            st_refs[j][...] = sts[j] * jnp.exp(exs[j][C - 1:C, :]) + lax.dot_general(
                vs[j], k_dec, tn, preferred_element_type=F32)
        for j, cs in enumerate(cols):
            o = os_[j]
            o = o * lax.rsqrt(jnp.mean(o * o, axis=-1, keepdims=True) + RMS_EPS)
            o = o * nw_ref[:, cs] * jax.nn.silu(og_ref[rows, cs].astype(F32))
            o_ref[rows, cs] = o.astype(o_ref.dtype)
        return carry

    lax.fori_loop(0, nchunk, chunk, 0)


def _hgrn(proj3, lb, norm_w, col_q, width):
    B, S, _ = proj3.shape
    hw = HG_DIM * HG_HEADS_PER_STEP
    ts = _tile(S, 1024)
    wall, mask = _hgrn_consts()
    nlev = mask.shape[0] - 1
    off = [(col_q + k * width) // hw for k in range(4)]

    def seg(k):
        return pl.BlockSpec((None, ts, hw), lambda b, h, s: (b, s, off[k] + h))

    kern = functools.partial(_hgrn_kernel, heads=HG_HEADS_PER_STEP, nchunk=ts // HG_CHUNK, nlev=nlev)
    return pl.pallas_call(
        kern,
        grid=(B, width // hw, S // ts),
        in_specs=[seg(0), seg(1), seg(2), seg(3),
                  pl.BlockSpec((1, hw), lambda b, h, s: (0, h)),
                  pl.BlockSpec((1, hw), lambda b, h, s: (0, h)),
                  pl.BlockSpec(wall.shape, lambda b, h, s: (0, 0)),
                  pl.BlockSpec(mask.shape, lambda b, h, s: (0, 0, 0))],
        out_specs=pl.BlockSpec((None, ts, hw), lambda b, h, s: (b, s, h)),
        out_shape=jax.ShapeDtypeStruct((B, S, width), BF16),
        scratch_shapes=[pltpu.VMEM((HG_DIM, HG_DIM), F32)] * HG_HEADS_PER_STEP,
        compiler_params=_params("parallel", "parallel", "arbitrary"),
        name="hgrn2",
    )(proj3, proj3, proj3, proj3, lb, norm_w, wall, mask)


def _mix_kernel(u_ref, o_ref, ga_ref, gb_ref, pw_ref, ps_ref, wa_ref, wb_ref,
                out_ref, pooled_ref, halo_ref, *, tiles_per_seq):
    i = pl.program_id(0)
    tm = u_ref.shape[0]
    gw = pw_ref.shape[1]

    @pl.when(pl.program_id(1) == 0)
    def _():
        @pl.when(i % tiles_per_seq == 0)
        def _():
            halo_ref[...] = jnp.zeros_like(halo_ref)

        u = u_ref[...].astype(F32)
        ext = jnp.concatenate([halo_ref[...], u], axis=0)
        halo_ref[...] = u[tm - POOL_HALO:, :]
        t1 = ((i % tiles_per_seq) * tm + 1 + lax.broadcasted_iota(I32, (tm, 1), 0)).astype(F32)
        for g, w in enumerate(POOL_WINDOWS):
            cs = slice(g * gw, (g + 1) * gw)
            s = ext[:, cs]
            shift = 1
            while shift < w:
                s = s + pltpu.roll(s, shift, 0)
                shift *= 2
            d = s[POOL_HALO:, :] * (1.0 / jnp.minimum(t1, float(w))) - u[:, cs]
            y = jnp.dot(d.astype(BF16), pw_ref[g], preferred_element_type=F32) * ps_ref[:, cs]
            pooled_ref[:, cs] = y.astype(BF16)

    ya = jnp.dot(pooled_ref[...], wa_ref[...], preferred_element_type=F32)
    yb = jnp.dot(o_ref[...], wb_ref[...], preferred_element_type=F32)
    mixed = (jax.nn.sigmoid(ga_ref[...].astype(F32)) * ya
             + jax.nn.sigmoid(gb_ref[...].astype(F32)) * yb)
    out_ref[...] = mixed.astype(out_ref.dtype)


def _mix(proj, o_b, pool_w, pool_scale, wa, wb, seq, col_ga, col_gb):
    T = proj.shape[0]
    pwid = wa.shape[0]
    D = wa.shape[1]
    tm = _tile(seq, 512)
    tn = _tile(D, 1024)
    kern = functools.partial(_mix_kernel, tiles_per_seq=seq // tm)
    return pl.pallas_call(
        kern,
        grid=(T // tm, D // tn),
        in_specs=[pl.BlockSpec((tm, pwid), lambda i, j: (i, 0)),
                  pl.BlockSpec((tm, o_b.shape[1]), lambda i, j: (i, 0)),
                  pl.BlockSpec((tm, tn), lambda i, j: (i, col_ga // tn + j)),
                  pl.BlockSpec((tm, tn), lambda i, j: (i, col_gb // tn + j)),
                  pl.BlockSpec(pool_w.shape, lambda i, j: (0, 0, 0)),
                  pl.BlockSpec((1, pwid), lambda i, j: (0, 0)),
                  pl.BlockSpec((pwid, tn), lambda i, j: (0, j)),
                  pl.BlockSpec((wb.shape[0], tn), lambda i, j: (0, j))],
        out_specs=pl.BlockSpec((tm, tn), lambda i, j: (i, j)),
        out_shape=jax.ShapeDtypeStruct((T, D), BF16),
        scratch_shapes=[pltpu.VMEM((tm, pwid), BF16), pltpu.VMEM((POOL_HALO, pwid), F32)],
        compiler_params=_params("arbitrary", "arbitrary"),
        name="pool_mix",
    )(proj, o_b, proj, proj, pool_w, pool_scale, wa, wb)


def _outproj_kernel(a_ref, w_ref, x_ref, o_ref, *, alpha):
    o_ref[...] = alpha * x_ref[...] + jnp.dot(a_ref[...], w_ref[...], preferred_element_type=F32)


def _outproj(mixed, w_out, x2, alpha):
    T, D = x2.shape
    tm, tn = _tile(T, 512), _tile(D, 1024)
    return pl.pallas_call(
        functools.partial(_outproj_kernel, alpha=alpha),
        grid=(T // tm, D // tn),
        in_specs=[pl.BlockSpec((tm, D), lambda i, j: (i, 0)),
                  pl.BlockSpec((D, tn), lambda i, j: (0, j)),
                  pl.BlockSpec((tm, tn), lambda i, j: (i, j))],
        out_specs=pl.BlockSpec((tm, tn), lambda i, j: (i, j)),
        out_shape=jax.ShapeDtypeStruct((T, D), F32),
        compiler_params=_params("parallel", "parallel"),
        name="out_proj",
    )(mixed, w_out, x2)


def _pack_halves(x):
    n = x.shape[1] // 2
    lo = pltpu.bitcast(x[:, :n].astype(BF16).astype(F32), U32)
    hi = pltpu.bitcast(x[:, n:].astype(BF16).astype(F32), U32)
    return (lo >> 16) | (hi & jnp.uint32(0xFFFF0000))


def _unpack_halves(u):
    lo = pltpu.bitcast(u << 16, F32)
    hi = pltpu.bitcast(u & jnp.uint32(0xFFFF0000), F32)
    return lo, hi


SUBLANES = 8


def _exchange(a, inner):
    n = a.shape[0]
    s_idx = lax.broadcasted_iota(I32, (1, 1, SUBLANES, LANES), 2)
    d = SUBLANES // 2
    while d >= 1:
        span = d * inner
        a4 = a.reshape(n // (2 * span), 2, span, SUBLANES, LANES)
        lo, hi = a4[:, 0], a4[:, 1]
        keep = (s_idx & d) == 0
        new_lo = jnp.where(keep, lo, pltpu.roll(hi, d, 2))
        new_hi = jnp.where(keep, pltpu.roll(lo, SUBLANES - d, 2), hi)
        a = jnp.stack([new_lo, new_hi], axis=1).reshape(n, SUBLANES, LANES)
        d //= 2
    return a


def _rows_to_tiles(x3):
    r, c, _ = x3.shape
    inner = c // SUBLANES
    a = _exchange(x3.reshape(r * inner, SUBLANES, LANES), inner)
    a = a.reshape(r // SUBLANES, SUBLANES, inner, SUBLANES, LANES)
    return jnp.concatenate([a[:, ci % SUBLANES, ci // SUBLANES].reshape(r, LANES) for ci in range(c)],
                           axis=1)


def _tiles_to_rows(x, c):
    r = x.shape[0]
    inner = c // SUBLANES
    g = r // SUBLANES
    cols = [x[:, ci * LANES:(ci + 1) * LANES].reshape(g, 1, 1, SUBLANES, LANES) for ci in range(c)]
    a = jnp.concatenate(
        [jnp.concatenate([cols[hi * SUBLANES + lo] for hi in range(inner)], axis=2)
         for lo in range(SUBLANES)], axis=1)
    a = _exchange(a.reshape(r * inner, SUBLANES, LANES), inner)
    return a.reshape(r, c, LANES)


def _ln1_kernel(pre_ref, g_ref, b_ref, rwh_ref, rwl_ref, x1_ref, x1b_ref, x1u_ref, lg_ref):
    x = pre_ref[...]
    mu = jnp.mean(x, axis=-1, keepdims=True)
    xc = x - mu
    var = jnp.mean(xc * xc, axis=-1, keepdims=True)
    y = xc * lax.rsqrt(var + LN_EPS) * g_ref[...] + b_ref[...]
    x1_ref[...] = y
    y_hi = y.astype(BF16)
    x1b_ref[...] = y_hi
    x1u_ref[...] = _tiles_to_rows(_pack_halves(y), x1u_ref.shape[1])
    y_lo = (y - y_hi.astype(F32)).astype(BF16)
    nt = (((1,), (1,)), ((), ()))
    lg = lax.dot_general(rwh_ref[...], y_hi, nt, preferred_element_type=F32)
    lg += lax.dot_general(rwh_ref[...], y_lo, nt, preferred_element_type=F32)
    lg += lax.dot_general(rwl_ref[...], y_hi, nt, preferred_element_type=F32)
    lg_ref[...] = lg


def _ln1(pre, g, b, rw_hi_t, rw_lo_t):
    T, D = pre.shape
    E = rw_hi_t.shape[0]
    tm = _tile(T, 256)
    row = lambda i: (i, 0)
    const = lambda i: (0, 0)
    return pl.pallas_call(
        _ln1_kernel,
        grid=(T // tm,),
        in_specs=[pl.BlockSpec((tm, D), row), pl.BlockSpec((1, D), const), pl.BlockSpec((1, D), const),
                  pl.BlockSpec((E, D), const), pl.BlockSpec((E, D), const)],
        out_specs=[pl.BlockSpec((tm, D), row), pl.BlockSpec((tm, D), row),
                   pl.BlockSpec((tm, D // 2 // LANES, LANES), lambda i: (i, 0, 0)),
                   pl.BlockSpec((E, tm), lambda i: (0, i))],
        out_shape=[jax.ShapeDtypeStruct((T, D), F32), jax.ShapeDtypeStruct((T, D), BF16),
                   jax.ShapeDtypeStruct((T, D // 2 // LANES, LANES), U32),
                   jax.ShapeDtypeStruct((E, T), F32)],
        compiler_params=_params("parallel"),
        name="ln1_router",
    )(pre, g, b, rw_hi_t, rw_lo_t)


def _route_kernel(lg_ref, bias_ref, tri_ref, idx_ref, w_ref, rank_ref, cnt_ref, carry_ref):
    @pl.when(pl.program_id(0) == 0)
    def _():
        carry_ref[...] = jnp.zeros_like(carry_ref)

    E, tr = lg_ref.shape
    gsz = E // N_GROUPS
    ninf = -jnp.inf
    scores = jax.nn.sigmoid(lg_ref[...])
    choice = scores + bias_ref[...]

    def first_max(x, n):
        io = lax.broadcasted_iota(I32, x.shape, 0)
        m = jnp.max(x, axis=0, keepdims=True)
        return m, jnp.min(jnp.where(x == m, io, n), axis=0, keepdims=True), io

    gs_rows = []
    for g in range(N_GROUPS):
        cg = choice[g * gsz:(g + 1) * gsz, :]
        m1, i1, io = first_max(cg, gsz)
        m2 = jnp.max(jnp.where(io == i1, ninf, cg), axis=0, keepdims=True)
        gs_rows.append(m1 + m2)
    gs = jnp.concatenate(gs_rows, axis=0)
    gsel = jnp.zeros(gs.shape, F32)
    for _ in range(TOPK_GROUPS):
        _, gi, io = first_max(gs, N_GROUPS)
        hit = io == gi
        gsel = jnp.where(hit, 1.0, gsel)
        gs = jnp.where(hit, ninf, gs)
    gsel_e = jnp.concatenate(
        [jnp.broadcast_to(gsel[g:g + 1, :], (gsz, tr)) for g in range(N_GROUPS)], axis=0)
    masked = jnp.where(gsel_e > 0.0, choice, ninf)

    idx_rows, w_rows = [], []
    member = jnp.zeros((E, tr), F32)
    for _ in range(TOP_K):
        _, ii, io = first_max(masked, E)
        hit = io == ii
        idx_rows.append(ii)
        w_rows.append(jnp.sum(jnp.where(hit, scores, 0.0), axis=0, keepdims=True))
        member = jnp.where(hit, 1.0, member)
        masked = jnp.where(hit, ninf, masked)
    wsum = w_rows[0]
    for wk in w_rows[1:]:
        wsum = wsum + wk
    idx_ref[...] = jnp.concatenate(idx_rows, axis=0)
    w_ref[...] = jnp.concatenate([wk / wsum * ROUTED_SCALE for wk in w_rows], axis=0)

    before = jnp.dot(member.astype(BF16), tri_ref[...], preferred_element_type=F32) + carry_ref[...]
    io = lax.broadcasted_iota(I32, (E, tr), 0)
    rank_ref[...] = jnp.concatenate(
        [jnp.sum(jnp.where(io == ii, before, 0.0), axis=0, keepdims=True) for ii in idx_rows],
        axis=0).astype(I32)
    carry_ref[...] += jnp.sum(member, axis=1, keepdims=True)
    cnt_ref[...] = carry_ref[...]


def _route(logits_t, bias):
    E, T = logits_t.shape
    tr = _tile(T, 512)
    tri = jnp.asarray(np.triu(np.ones((tr, tr), np.float32), k=1), BF16)
    col = lambda i: (0, i)
    return pl.pallas_call(
        _route_kernel,
        grid=(T // tr,),
        in_specs=[pl.BlockSpec((E, tr), col), pl.BlockSpec((E, 1), lambda i: (0, 0)),
                  pl.BlockSpec((tr, tr), lambda i: (0, 0))],
        out_specs=[pl.BlockSpec((TOP_K, tr), col), pl.BlockSpec((TOP_K, tr), col),
                   pl.BlockSpec((TOP_K, tr), col), pl.BlockSpec((E, 1), lambda i: (0, 0))],
        out_shape=[jax.ShapeDtypeStruct((TOP_K, T), I32), jax.ShapeDtypeStruct((TOP_K, T), F32),
                   jax.ShapeDtypeStruct((TOP_K, T), I32), jax.ShapeDtypeStruct((E, 1), F32)],
        scratch_shapes=[pltpu.VMEM((E, 1), F32)],
        compiler_params=_params("arbitrary"),
        name="route",
    )(logits_t, bias, tri)


DISPATCH_TOKENS = SMEM_INDEX_CHUNK // TOP_K
PAD_SIZES = tuple(1 << s for s in reversed(range((EXPERT_BLOCK - 1).bit_length())))
assert PAD_SIZES[0] <= DISPATCH_TOKENS


def _dispatch_rows(step, tok0, pad_start_ref, pad_len_ref, pos_hbm, x_ref, xs_hbm,
                   idx_smem, idx_sem, row_sem, pad_sem):
    n = idx_smem.shape[0]
    base = pl.multiple_of(step * n, SMEM_INDEX_CHUNK)
    idx_cp = pltpu.make_async_copy(pos_hbm.at[pl.ds(base, n)], idx_smem, idx_sem)
    idx_cp.start()
    idx_cp.wait()

    def row_copy(t, dst_row):
        return pltpu.make_async_copy(x_ref.at[t], xs_hbm.at[dst_row], row_sem)

    def issue(t, c):
        for k in range(TOP_K):
            row_copy(tok0 + t, idx_smem[t * TOP_K + k]).start()
        return c

    def drain(r, c):
        row_copy(0, 0).wait()
        return c

    lax.fori_loop(0, n // TOP_K, issue, 0)

    @pl.when(step == 0)
    def _():
        def pad_pass(act):
            def per_expert(e, c):
                ln = pad_len_ref[e]
                off = pad_start_ref[e]
                for sz in PAD_SIZES:
                    @pl.when((ln & sz) != 0)
                    def _():
                        dst = off + (ln & ~(2 * sz - 1))
                        act(pltpu.make_async_copy(x_ref.at[pl.ds(0, sz)], xs_hbm.at[pl.ds(dst, sz)],
                                                  pad_sem))
                return c
            lax.fori_loop(0, pad_len_ref.shape[0], per_expert, 0)

            run = PAD_SIZES[0]
            first = pad_start_ref[pad_len_ref.shape[0]]

            def per_run(q, c):
                act(pltpu.make_async_copy(x_ref.at[pl.ds(0, run)],
                                          xs_hbm.at[pl.ds(first + q * run, run)], pad_sem))
                return c
            lax.fori_loop(0, (xs_hbm.shape[0] - first) // run, per_run, 0)

        pad_pass(lambda cp: cp.start())
        pad_pass(lambda cp: cp.wait())

    return lambda: lax.fori_loop(0, n, drain, 0, unroll=DMA_LOOP_UNROLL)


def _expert_kernel(be_ref, nxt_ref, nu_ref, xs_ref, w1_hbm, w3_hbm, w2_hbm, y_ref,
                   w1f, w3f, w2f, w1b, w3b, w2b, slot_ref, wsem):
    n = pl.program_id(0)
    used = n < nu_ref[0]

    @pl.when(jnp.logical_not(used))
    def _():
        y_ref[...] = jnp.zeros_like(y_ref)

    def weight_copies(e, s):
        return [pltpu.make_async_copy(src.at[e], dst.at[s], wsem.at[s])
                for src, dst in ((w1_hbm, w1f), (w3_hbm, w3f), (w2_hbm, w2f))]

    @pl.when(used)
    def _():
        prev = be_ref[jnp.maximum(n - 1, 0)]

        @pl.when((n == 0) | (be_ref[n] != prev))
        def _():
            @pl.when(n == 0)
            def _():
                slot_ref[0] = 1
                for cp in weight_copies(be_ref[0], 0):
                    cp.start()

            s = 1 - slot_ref[0]
            slot_ref[0] = s
            for cp in weight_copies(0, s):
                cp.wait()

            @pl.when(nxt_ref[n] >= 0)
            def _():
                for cp in weight_copies(nxt_ref[n], 1 - s):
                    cp.start()

            w1b[...] = w1f[s].astype(BF16)
            w3b[...] = w3f[s].astype(BF16)
            w2b[...] = w2f[s].astype(BF16)

        lo, hi = _unpack_halves(_rows_to_tiles(xs_ref[...]))
        half = lo.shape[1]
        lo, hi = lo.astype(BF16), hi.astype(BF16)
        h1 = (jnp.dot(lo, w1b[:half, :], preferred_element_type=F32)
              + jnp.dot(hi, w1b[half:, :], preferred_element_type=F32))
        h3 = (jnp.dot(lo, w3b[:half, :], preferred_element_type=F32)
              + jnp.dot(hi, w3b[half:, :], preferred_element_type=F32))
        hdn = (jax.nn.silu(h1) * h3).astype(BF16)
        y = _pack_halves(jnp.dot(hdn, w2b[...], preferred_element_type=F32))
        y_ref[...] = _tiles_to_rows(y, y_ref.shape[1])


def _experts(block_e, next_e, n_used, xs, w1, w3, w2):
    P, S, L = xs.shape
    E, D, H = w1.shape
    nb = P // EXPERT_BLOCK
    grid_spec = pltpu.PrefetchScalarGridSpec(
        num_scalar_prefetch=3,
        grid=(nb,),
        in_specs=[pl.BlockSpec((EXPERT_BLOCK, S, L),
                               lambda n, be, nx, nu: (jnp.minimum(n, nu[0] - 1), 0, 0)),
                  pl.BlockSpec(memory_space=pl.ANY), pl.BlockSpec(memory_space=pl.ANY),
                  pl.BlockSpec(memory_space=pl.ANY)],
        out_specs=pl.BlockSpec((EXPERT_BLOCK, S, L), lambda n, be, nx, nu: (n, 0, 0)),
        scratch_shapes=[pltpu.VMEM((2, D, H), F32), pltpu.VMEM((2, D, H), F32), pltpu.VMEM((2, H, D), F32),
                        pltpu.VMEM((D, H), BF16), pltpu.VMEM((D, H), BF16), pltpu.VMEM((H, D), BF16),
                        pltpu.SMEM((1,), I32), pltpu.SemaphoreType.DMA((2,))],
    )
    return pl.pallas_call(
        _expert_kernel,
        grid_spec=grid_spec,
        out_shape=jax.ShapeDtypeStruct((P, S, L), U32),
        compiler_params=pltpu.CompilerParams(dimension_semantics=("arbitrary",),
                                             vmem_limit_bytes=EXPERT_VMEM_LIMIT),
        name="experts",
    )(block_e, next_e, n_used, xs, w1, w3, w2)


def _addend_kernel(pad_start_ref, pad_len_ref, pos_hbm, xu_ref, x_ref, p_ref, s1_ref, s3_ref, s2_ref,
                   wg_ref, wp_ref, o_ref, xs_hbm, h_ref, idx_smem, idx_sem, row_sem, pad_sem):
    j = pl.program_id(1)
    step = pl.program_id(0) * pl.num_programs(1) + j
    wait_rows = _dispatch_rows(step, j * DISPATCH_TOKENS, pad_start_ref, pad_len_ref, pos_hbm, xu_ref,
                               xs_hbm, idx_smem, idx_sem, row_sem, pad_sem)

    @pl.when(j == 0)
    def _():
        x = x_ref[...]
        h1 = jnp.dot(x, s1_ref[...], preferred_element_type=F32)
        h3 = jnp.dot(x, s3_ref[...], preferred_element_type=F32)
        h_ref[...] = (jax.nn.silu(h1) * h3).astype(BF16)

    shared = jnp.dot(h_ref[...], s2_ref[...], preferred_element_type=F32)
    gate = jax.nn.sigmoid(jnp.dot(x_ref[...], wg_ref[...], preferred_element_type=F32))
    emb = jnp.dot(p_ref[...], wp_ref[...], preferred_element_type=F32)
    o_ref[...] = (shared + gate * emb).astype(o_ref.dtype)
    wait_rows()


def _addend_dispatch(pad_start, pad_len, pos_tok_major, x1u, n_rows, x1b, p2, s1, s3, s2, wg, wp):
    T, D = x1b.shape
    _, S, L = x1u.shape
    H = s1.shape[1]
    PD = p2.shape[1]
    tn = _tile(D, 1024)
    tm = DISPATCH_TOKENS * (D // tn)
    assert T % tm == 0
    grid_spec = pltpu.PrefetchScalarGridSpec(
        num_scalar_prefetch=2,
        grid=(T // tm, D // tn),
        in_specs=[pl.BlockSpec(memory_space=pl.ANY),
                  pl.BlockSpec((tm, S, L), lambda i, j, ps, pn: (i, 0, 0)),
                  pl.BlockSpec((tm, D), lambda i, j, ps, pn: (i, 0)),
                  pl.BlockSpec((tm, PD), lambda i, j, ps, pn: (i, 0)),
                  pl.BlockSpec((D, H), lambda i, j, ps, pn: (0, 0)),
                  pl.BlockSpec((D, H), lambda i, j, ps, pn: (0, 0)),
                  pl.BlockSpec((H, tn), lambda i, j, ps, pn: (0, j)),
                  pl.BlockSpec((D, tn), lambda i, j, ps, pn: (0, j)),
                  pl.BlockSpec((PD, tn), lambda i, j, ps, pn: (0, j))],
        out_specs=[pl.BlockSpec((tm, tn), lambda i, j, ps, pn: (i, j)),
                   pl.BlockSpec(memory_space=pl.ANY)],
        scratch_shapes=[pltpu.VMEM((tm, H), BF16), pltpu.SMEM((SMEM_INDEX_CHUNK,), I32),
                        pltpu.SemaphoreType.DMA(()), pltpu.SemaphoreType.DMA(()),
                        pltpu.SemaphoreType.DMA(())],
    )
    return pl.pallas_call(
        _addend_kernel,
        grid_spec=grid_spec,
        out_shape=[jax.ShapeDtypeStruct((T, D), BF16), jax.ShapeDtypeStruct((n_rows, S, L), x1u.dtype)],
        compiler_params=_params("arbitrary", "arbitrary"),
        name="shared_ple_dispatch",
    )(pad_start, pad_len, pos_tok_major, x1u, x1b, p2, s1, s3, s2, wg, wp)


def _final_kernel(pos_hbm, y_hbm, x1_ref, add_ref, w_ref, g_ref, b_ref, o_ref,
                  buf, idx_smem, idx_sem, row_sems, *, alpha):
    i = pl.program_id(0)
    tm = x1_ref.shape[0]
    n = tm * TOP_K

    def row_copy(slot, r, src_row):
        return pltpu.make_async_copy(y_hbm.at[src_row], buf.at[slot, r], row_sems.at[slot])

    def fetch(step, slot):
        base = pl.multiple_of(step * n, SMEM_INDEX_CHUNK)
        idx_cp = pltpu.make_async_copy(pos_hbm.at[pl.ds(base, n)], idx_smem, idx_sem)
        idx_cp.start()
        idx_cp.wait()

        def issue(r, c):
            row_copy(slot, r, idx_smem[r]).start()
            return c

        lax.fori_loop(0, n, issue, 0, unroll=DMA_LOOP_UNROLL)

    @pl.when(i == 0)
    def _():
        fetch(0, 0)

    @pl.when(i + 1 < pl.num_programs(0))
    def _():
        fetch(i + 1, (i + 1) % 2)

    slot = i % 2

    def drain(r, c):
        row_copy(slot, 0, 0).wait()
        return c

    lax.fori_loop(0, n, drain, 0, unroll=DMA_LOOP_UNROLL)

    acc_lo = jnp.zeros((tm,) + buf.shape[2:], F32)
    acc_hi = acc_lo
    for k in range(TOP_K):
        lo, hi = _unpack_halves(buf[slot, k * tm:(k + 1) * tm])
        wk = w_ref[:, k:k + 1, :]
        acc_lo += wk * lo
        acc_hi += wk * hi
    routed = jnp.concatenate([_rows_to_tiles(acc_lo), _rows_to_tiles(acc_hi)], axis=1)
    pre = alpha * x1_ref[...] + add_ref[...].astype(F32) + routed
    mu = jnp.mean(pre, axis=-1, keepdims=True)
    xc = pre - mu
    var = jnp.mean(xc * xc, axis=-1, keepdims=True)
    o_ref[...] = xc * lax.rsqrt(var + LN_EPS) * g_ref[...] + b_ref[...]


def _final(pos_flat, y_sorted, x1, addend, w_tok, g, b, alpha):
    T, D = x1.shape
    _, S, L = y_sorted.shape
    tm = SMEM_INDEX_CHUNK // TOP_K
    row = lambda i: (i, 0)
    const = lambda i: (0, 0)
    return pl.pallas_call(
        functools.partial(_final_kernel, alpha=alpha),
        grid=(T // tm,),
        in_specs=[pl.BlockSpec(memory_space=pl.ANY), pl.BlockSpec(memory_space=pl.ANY),
                  pl.BlockSpec((tm, D), row), pl.BlockSpec((tm, D), row),
                  pl.BlockSpec((tm, TOP_K, LANES), lambda i: (i, 0, 0)),
                  pl.BlockSpec((1, D), const), pl.BlockSpec((1, D), const)],
        out_specs=pl.BlockSpec((tm, D), row),
        out_shape=jax.ShapeDtypeStruct((T, D), F32),
        scratch_shapes=[pltpu.VMEM((2, tm * TOP_K, S, L), U32), pltpu.SMEM((tm * TOP_K,), I32),
                        pltpu.SemaphoreType.DMA(()), pltpu.SemaphoreType.DMA((2,))],
        compiler_params=_params("arbitrary"),
        name="combine_ln2",
    )(pos_flat, y_sorted, x1, addend, w_tok, g, b)


def _dispatch_plan(top_idx, rank, counts, T):
    E = counts.shape[0]
    blk = EXPERT_BLOCK
    n_blocks = -(-(T * TOP_K) // blk) + E
    padded = (counts + blk - 1) // blk * blk
    ends = jnp.cumsum(padded)
    starts = ends - padded
    onehot = top_idx[None, :, :] == jnp.arange(E, dtype=I32)[:, None, None]
    pos = jnp.sum(jnp.where(onehot, starts[:, None, None], 0), axis=0) + rank
    first_row = jnp.arange(n_blocks, dtype=I32) * blk
    block_e = jnp.minimum(jnp.sum(ends[None, :] <= first_row[:, None], axis=1), E - 1).astype(I32)
    n_used = (ends[-1:] // blk).astype(I32)
    bidx = jnp.arange(n_blocks, dtype=I32)
    run_start = (bidx > 0) & (bidx < n_used[0]) & (block_e != jnp.roll(block_e, 1))
    nxt_start = lax.cummin(jnp.where(run_start, bidx, n_blocks), reverse=True)
    nxt_start = jnp.concatenate([nxt_start[1:], jnp.full((1,), n_blocks, I32)])
    next_e = jnp.where(nxt_start < n_blocks, block_e[jnp.minimum(nxt_start, n_blocks - 1)], -1)
    pad_start = jnp.concatenate([starts + counts, ends[-1:]])
    return pos, pad_start, padded - counts, block_e, next_e.astype(I32), n_used, n_blocks * blk


def kernel(x, p, w_in, pool_w, pool_scale, lb_param, hg_norm_w, w_branch_a, w_branch_b, w_out, ln1_g, ln1_b, router_w, router_bias, exp_w1, exp_w3, exp_w2, sh_w1, sh_w3, sh_w2, ple_gate_w, ple_proj_w, ln2_g, ln2_b):
    B, S, D = x.shape
    T = B * S
    depth = w_in.shape[0]
    alpha = (2.0 * depth) ** 0.25
    pool_width = w_branch_a.shape[1]
    hg_width = w_branch_b.shape[1]
    col_q = pool_width
    col_ga = pool_width + 4 * hg_width
    col_gb = col_ga + D
    lb_all = jnp.cumsum(jax.nn.softmax(lb_param.astype(F32), axis=0), axis=0)

    for i in range(depth):
        x2 = x.reshape(T, D)
        proj = _matmul(x2.astype(BF16), w_in[i].astype(F32), BF16, 1024, 512)
        o_b = _hgrn(proj.reshape(B, S, -1), lb_all[i][None, :], hg_norm_w[i][None, :].astype(F32),
                    col_q, hg_width).reshape(T, hg_width)
        mixed = _mix(proj, o_b, pool_w[i].astype(BF16), pool_scale[i][None, :].astype(F32),
                     w_branch_a[i].astype(BF16), w_branch_b[i].astype(BF16), S, col_ga, col_gb)
        pre = _outproj(mixed, w_out[i].astype(BF16), x2, alpha)

        rw_t = router_w[i].astype(F32).T
        rw_hi = rw_t.astype(BF16)
        rw_lo = (rw_t - rw_hi.astype(F32)).astype(BF16)
        x1, x1b, x1u, logits_t = _ln1(pre, ln1_g[i][None, :], ln1_b[i][None, :], rw_hi, rw_lo)

        top_idx, gate_w, rank, counts = _route(logits_t, router_bias[i].astype(F32)[:, None])
        pos, pad_start, pad_len, block_e, next_e, n_used, n_rows = _dispatch_plan(
            top_idx, rank, counts[:, 0].astype(I32), T)
        addend, xs = _addend_dispatch(
            pad_start, pad_len, pos.T.reshape(-1), x1u, n_rows,
            x1b, p[i].reshape(T, -1).astype(BF16), sh_w1[i].astype(BF16), sh_w3[i].astype(BF16),
            sh_w2[i].astype(BF16), ple_gate_w[i].astype(BF16), ple_proj_w[i].astype(BF16))
        ys = _experts(block_e, next_e, n_used, xs, exp_w1[i].astype(F32), exp_w3[i].astype(F32),
                      exp_w2[i].astype(F32))

        tm = SMEM_INDEX_CHUNK // TOP_K
        pos_flat = pos.reshape(TOP_K, T // tm, tm).transpose(1, 0, 2).reshape(-1)
        w_rep = jnp.broadcast_to(gate_w.T[:, :, None], (T, TOP_K, LANES))
        out = _final(pos_flat, ys, x1, addend, w_rep, ln2_g[i][None, :], ln2_b[i][None, :], alpha)
        x = out.reshape(B, S, D)
    return x
```

```python
import functools

import numpy as np
import jax
import jax.numpy as jnp
from jax import lax
from jax.experimental import pallas as pl
from jax.experimental.pallas import tpu as pltpu

F32 = jnp.float32
BF16 = jnp.bfloat16
U32 = jnp.uint32
I32 = jnp.int32

V7X_VMEM_BYTES = 64 * 1024 * 1024
VMEM_LIMIT = V7X_VMEM_BYTES - 12 * 1024 * 1024
EXPERT_VMEM_LIMIT = V7X_VMEM_BYTES - 6 * 1024 * 1024
LANES = 128
SMEM_INDEX_CHUNK = 1024

POOL_WINDOWS = (2, 4, 8, 16)
HG_DIM = 128
N_GROUPS = 8
TOPK_GROUPS = 4
TOP_K = 8
ROUTED_SCALE = 2.5
LN_EPS = 1e-5
RMS_EPS = 1e-6

HG_CHUNK = 64
HG_HEADS_PER_STEP = 8
EXPERT_BLOCK = 256
POOL_HALO = 16
DMA_LOOP_UNROLL = 8


def _tile(n, pref):
    t = min(n, pref)
    while n % t:
        t -= 1
    return t


def _params(*sem):
    return pltpu.CompilerParams(dimension_semantics=sem, vmem_limit_bytes=VMEM_LIMIT)


def _mm_kernel(a_ref, b_ref, o_ref):
    o_ref[...] = jnp.dot(a_ref[...], b_ref[...], preferred_element_type=F32).astype(o_ref.dtype)


def _matmul(a, b, out_dtype, tm, tn):
    M, K = a.shape
    N = b.shape[1]
    tm, tn = _tile(M, tm), _tile(N, tn)
    return pl.pallas_call(
        _mm_kernel,
        grid=(M // tm, N // tn),
        in_specs=[pl.BlockSpec((tm, K), lambda i, j: (i, 0)),
                  pl.BlockSpec((K, tn), lambda i, j: (0, j))],
        out_specs=pl.BlockSpec((tm, tn), lambda i, j: (i, j)),
        out_shape=jax.ShapeDtypeStruct((M, N), out_dtype),
        compiler_params=_params("parallel", "parallel"),
        name="in_proj",
    )(a, b)


def _hgrn_consts():
    C = HG_CHUNK
    t = np.arange(C)[:, None]
    u = np.arange(C)[None, :]
    blocks = [u <= t, u > t]
    masks = []
    h = C // 2
    while h >= 1:
        same = (t // (2 * h)) == (u // (2 * h))
        t2 = (t % (2 * h)) >= h
        u2 = (u % (2 * h)) >= h
        blocks.append(same & ((t2 & u2 & (u <= t)) | (~t2 & ~u2 & (u > t))))
        masks.append(same & t2 & ~u2)
        h //= 2
    masks.append(t == u)
    wall = np.concatenate(blocks, axis=0).astype(np.float32)
    mask = np.stack(masks).astype(np.float32)
    return jnp.asarray(wall, BF16), jnp.asarray(mask, F32)


def _hgrn_kernel(q_ref, f_ref, i_ref, og_ref, lb_ref, nw_ref, wall_ref, mask_ref,
                 o_ref, *st_refs, heads, nchunk, nlev):
    C = HG_CHUNK
    D = HG_DIM

    @pl.when(pl.program_id(2) == 0)
    def _():
        for st_ref in st_refs:
            st_ref[...] = jnp.zeros_like(st_ref)

    nt = (((1,), (1,)), ((), ()))
    tn = (((0,), (0,)), ((), ()))

    def chunk(c, carry):
        r0 = pl.multiple_of(c * C, C)
        rows = pl.ds(r0, C)
        hs = range(heads)
        cols = [slice(j * D, (j + 1) * D) for j in hs]
        qs, ks, vs, gcat = [], [], [], []
        for cs in cols:
            lb = lb_ref[:, cs]
            fl = f_ref[rows, cs].astype(F32)
            e = jnp.exp(-jnp.abs(fl))
            r = 1.0 / (1.0 + e)
            pos = fl >= 0
            sig_p = jnp.where(pos, r, e * r)
            sig_n = jnp.where(pos, e * r, r)
            g = jnp.log(lb + (1.0 - lb) * sig_p)
            ks.append((1.0 - lb) * sig_n)
            qs.append(jax.nn.silu(q_ref[rows, cs].astype(F32)))
            vs.append(i_ref[rows, cs])
            g_hi = g.astype(BF16)
            gcat += [g_hi, (g - g_hi.astype(F32)).astype(BF16)]
        e2 = jnp.dot(wall_ref[...], jnp.concatenate(gcat, axis=1), preferred_element_type=F32)
        exs = [e2[:, 2 * j * D:(2 * j + 1) * D] + e2[:, (2 * j + 1) * D:(2 * j + 2) * D] for j in hs]
        sts = [st_refs[j][...] for j in hs]
        o_inter = [lax.dot_general((qs[j] * jnp.exp(exs[j][0:C])).astype(BF16), sts[j].astype(BF16), nt,
                                   preferred_element_type=F32) for j in hs]
        a = [lax.dot_general(qs[j].astype(BF16), ks[j].astype(BF16), nt,
                             preferred_element_type=F32) * mask_ref[nlev] for j in hs]
        for l in range(nlev):
            for j in hs:
                xl = jnp.exp(exs[j][(2 + l) * C:(3 + l) * C])
                a[j] = a[j] + lax.dot_general((qs[j] * xl).astype(BF16), (ks[j] * xl).astype(BF16), nt,
                                              preferred_element_type=F32) * mask_ref[l]
        os_ = [o_inter[j] + jnp.dot(a[j].astype(BF16), vs[j], preferred_element_type=F32) for j in hs]
        for j in hs:
            k_dec = (ks[j] * jnp.exp(exs[j][C:2 * C])).astype(BF16)
            st_refs[j][...] = sts[j] * jnp.exp(exs[j][C - 1:C, :]) + lax.dot_general(
                vs[j], k_dec, tn, preferred_element_type=F32)
        for j, cs in enumerate(cols):
            o = os_[j]
            o = o * lax.rsqrt(jnp.mean(o * o, axis=-1, keepdims=True) + RMS_EPS)
            o = o * nw_ref[:, cs] * jax.nn.silu(og_ref[rows, cs].astype(F32))
            o_ref[rows, cs] = o.astype(o_ref.dtype)
        return carry

    lax.fori_loop(0, nchunk, chunk, 0)


def _hgrn(proj3, lb, norm_w, col_q, width):
    B, S, _ = proj3.shape
    hw = HG_DIM * HG_HEADS_PER_STEP
    ts = _tile(S, 1024)
    wall, mask = _hgrn_consts()
    nlev = mask.shape[0] - 1
    off = [(col_q + k * width) // hw for k in range(4)]

    def seg(k):
        return pl.BlockSpec((None, ts, hw), lambda b, h, s: (b, s, off[k] + h))

    kern = functools.partial(_hgrn_kernel, heads=HG_HEADS_PER_STEP, nchunk=ts // HG_CHUNK, nlev=nlev)
    return pl.pallas_call(
        kern,
        grid=(B, width // hw, S // ts),
        in_specs=[seg(0), seg(1), seg(2), seg(3),
                  pl.BlockSpec((1, hw), lambda b, h, s: (0, h)),
                  pl.BlockSpec((1, hw), lambda b, h, s: (0, h)),
                  pl.BlockSpec(wall.shape, lambda b, h, s: (0, 0)),
                  pl.BlockSpec(mask.shape, lambda b, h, s: (0, 0, 0))],
        out_specs=pl.BlockSpec((None, ts, hw), lambda b, h, s: (b, s, h)),
        out_shape=jax.ShapeDtypeStruct((B, S, width), BF16),
        scratch_shapes=[pltpu.VMEM((HG_DIM, HG_DIM), F32)] * HG_HEADS_PER_STEP,
        compiler_params=_params("parallel", "parallel", "arbitrary"),
        name="hgrn2",
    )(proj3, proj3, proj3, proj3, lb, norm_w, wall, mask)


def _mix_kernel(u_ref, o_ref, ga_ref, gb_ref, pw_ref, ps_ref, wa_ref, wb_ref,
                out_ref, pooled_ref, halo_ref, *, tiles_per_seq):
    i = pl.program_id(0)
    tm = u_ref.shape[0]
    gw = pw_ref.shape[1]

    @pl.when(pl.program_id(1) == 0)
    def _():
        @pl.when(i % tiles_per_seq == 0)
        def _():
            halo_ref[...] = jnp.zeros_like(halo_ref)

        u = u_ref[...].astype(F32)
        ext = jnp.concatenate([halo_ref[...], u], axis=0)
        halo_ref[...] = u[tm - POOL_HALO:, :]
        t1 = ((i % tiles_per_seq) * tm + 1 + lax.broadcasted_iota(I32, (tm, 1), 0)).astype(F32)
        for g, w in enumerate(POOL_WINDOWS):
            cs = slice(g * gw, (g + 1) * gw)
            s = ext[:, cs]
            shift = 1
            while shift < w:
                s = s + pltpu.roll(s, shift, 0)
                shift *= 2
            d = s[POOL_HALO:, :] * (1.0 / jnp.minimum(t1, float(w))) - u[:, cs]
            y = jnp.dot(d.astype(BF16), pw_ref[g], preferred_element_type=F32) * ps_ref[:, cs]
            pooled_ref[:, cs] = y.astype(BF16)

    ya = jnp.dot(pooled_ref[...], wa_ref[...], preferred_element_type=F32)
    yb = jnp.dot(o_ref[...], wb_ref[...], preferred_element_type=F32)
    mixed = (jax.nn.sigmoid(ga_ref[...].astype(F32)) * ya
             + jax.nn.sigmoid(gb_ref[...].astype(F32)) * yb)
    out_ref[...] = mixed.astype(out_ref.dtype)


def _mix(proj, o_b, pool_w, pool_scale, wa, wb, seq, col_ga, col_gb):
    T = proj.shape[0]
    pwid = wa.shape[0]
    D = wa.shape[1]
    tm = _tile(seq, 512)
    tn = _tile(D, 1024)
    kern = functools.partial(_mix_kernel, tiles_per_seq=seq // tm)
    return pl.pallas_call(
        kern,
        grid=(T // tm, D // tn),
        in_specs=[pl.BlockSpec((tm, pwid), lambda i, j: (i, 0)),
                  pl.BlockSpec((tm, o_b.shape[1]), lambda i, j: (i, 0)),
                  pl.BlockSpec((tm, tn), lambda i, j: (i, col_ga // tn + j)),
                  pl.BlockSpec((tm, tn), lambda i, j: (i, col_gb // tn + j)),
                  pl.BlockSpec(pool_w.shape, lambda i, j: (0, 0, 0)),
                  pl.BlockSpec((1, pwid), lambda i, j: (0, 0)),
                  pl.BlockSpec((pwid, tn), lambda i, j: (0, j)),
                  pl.BlockSpec((wb.shape[0], tn), lambda i, j: (0, j))],
        out_specs=pl.BlockSpec((tm, tn), lambda i, j: (i, j)),
        out_shape=jax.ShapeDtypeStruct((T, D), BF16),
        scratch_shapes=[pltpu.VMEM((tm, pwid), BF16), pltpu.VMEM((POOL_HALO, pwid), F32)],
        compiler_params=_params("arbitrary", "arbitrary"),
        name="pool_mix",
    )(proj, o_b, proj, proj, pool_w, pool_scale, wa, wb)


def _outproj_kernel(a_ref, w_ref, x_ref, o_ref, *, alpha):
    o_ref[...] = alpha * x_ref[...] + jnp.dot(a_ref[...], w_ref[...], preferred_element_type=F32)


def _outproj(mixed, w_out, x2, alpha):
    T, D = x2.shape
    tm, tn = _tile(T, 512), _tile(D, 1024)
    return pl.pallas_call(
        functools.partial(_outproj_kernel, alpha=alpha),
        grid=(T // tm, D // tn),
        in_specs=[pl.BlockSpec((tm, D), lambda i, j: (i, 0)),
                  pl.BlockSpec((D, tn), lambda i, j: (0, j)),
                  pl.BlockSpec((tm, tn), lambda i, j: (i, j))],
        out_specs=pl.BlockSpec((tm, tn), lambda i, j: (i, j)),
        out_shape=jax.ShapeDtypeStruct((T, D), F32),
        compiler_params=_params("parallel", "parallel"),
        name="out_proj",
    )(mixed, w_out, x2)


def _pack_halves(x):
    n = x.shape[1] // 2
    lo = pltpu.bitcast(x[:, :n].astype(BF16).astype(F32), U32)
    hi = pltpu.bitcast(x[:, n:].astype(BF16).astype(F32), U32)
    return (lo >> 16) | (hi & jnp.uint32(0xFFFF0000))


def _unpack_halves(u):
    lo = pltpu.bitcast(u << 16, F32)
    hi = pltpu.bitcast(u & jnp.uint32(0xFFFF0000), F32)
    return lo, hi


SUBLANES = 8


def _exchange(a, inner):
    n = a.shape[0]
    s_idx = lax.broadcasted_iota(I32, (1, 1, SUBLANES, LANES), 2)
    d = SUBLANES // 2
    while d >= 1:
        span = d * inner
        a4 = a.reshape(n // (2 * span), 2, span, SUBLANES, LANES)
        lo, hi = a4[:, 0], a4[:, 1]
        keep = (s_idx & d) == 0
        new_lo = jnp.where(keep, lo, pltpu.roll(hi, d, 2))
        new_hi = jnp.where(keep, pltpu.roll(lo, SUBLANES - d, 2), hi)
        a = jnp.stack([new_lo, new_hi], axis=1).reshape(n, SUBLANES, LANES)
        d //= 2
    return a


def _rows_to_tiles(x3):
    r, c, _ = x3.shape
    inner = c // SUBLANES
    a = _exchange(x3.reshape(r * inner, SUBLANES, LANES), inner)
    a = a.reshape(r // SUBLANES, SUBLANES, inner, SUBLANES, LANES)
    return jnp.concatenate([a[:, ci % SUBLANES, ci // SUBLANES].reshape(r, LANES) for ci in range(c)],
                           axis=1)


def _tiles_to_rows(x, c):
    r = x.shape[0]
    inner = c // SUBLANES
    g = r // SUBLANES
    cols = [x[:, ci * LANES:(ci + 1) * LANES].reshape(g, 1, 1, SUBLANES, LANES) for ci in range(c)]
    a = jnp.concatenate(
        [jnp.concatenate([cols[hi * SUBLANES + lo] for hi in range(inner)], axis=2)
         for lo in range(SUBLANES)], axis=1)
    a = _exchange(a.reshape(r * inner, SUBLANES, LANES), inner)
    return a.reshape(r, c, LANES)


def _ln1_kernel(pre_ref, g_ref, b_ref, rwh_ref, rwl_ref, x1_ref, x1b_ref, x1u_ref, lg_ref):
    x = pre_ref[...]
    mu = jnp.mean(x, axis=-1, keepdims=True)
    xc = x - mu
    var = jnp.mean(xc * xc, axis=-1, keepdims=True)
    y = xc * lax.rsqrt(var + LN_EPS) * g_ref[...] + b_ref[...]
    x1_ref[...] = y
    y_hi = y.astype(BF16)
    x1b_ref[...] = y_hi
    x1u_ref[...] = _tiles_to_rows(_pack_halves(y), x1u_ref.shape[1])
    y_lo = (y - y_hi.astype(F32)).astype(BF16)
    nt = (((1,), (1,)), ((), ()))
    lg = lax.dot_general(rwh_ref[...], y_hi, nt, preferred_element_type=F32)
    lg += lax.dot_general(rwh_ref[...], y_lo, nt, preferred_element_type=F32)
    lg += lax.dot_general(rwl_ref[...], y_hi, nt, preferred_element_type=F32)
    lg_ref[...] = lg


def _ln1(pre, g, b, rw_hi_t, rw_lo_t):
    T, D = pre.shape
    E = rw_hi_t.shape[0]
    tm = _tile(T, 256)
    row = lambda i: (i, 0)
    const = lambda i: (0, 0)
    return pl.pallas_call(
        _ln1_kernel,
        grid=(T // tm,),
        in_specs=[pl.BlockSpec((tm, D), row), pl.BlockSpec((1, D), const), pl.BlockSpec((1, D), const),
                  pl.BlockSpec((E, D), const), pl.BlockSpec((E, D), const)],
        out_specs=[pl.BlockSpec((tm, D), row), pl.BlockSpec((tm, D), row),
                   pl.BlockSpec((tm, D // 2 // LANES, LANES), lambda i: (i, 0, 0)),
                   pl.BlockSpec((E, tm), lambda i: (0, i))],
        out_shape=[jax.ShapeDtypeStruct((T, D), F32), jax.ShapeDtypeStruct((T, D), BF16),
                   jax.ShapeDtypeStruct((T, D // 2 // LANES, LANES), U32),
                   jax.ShapeDtypeStruct((E, T), F32)],
        compiler_params=_params("parallel"),
        name="ln1_router",
    )(pre, g, b, rw_hi_t, rw_lo_t)


def _route_kernel(lg_ref, bias_ref, tri_ref, idx_ref, w_ref, rank_ref, cnt_ref, carry_ref):
    @pl.when(pl.program_id(0) == 0)
    def _():
        carry_ref[...] = jnp.zeros_like(carry_ref)

    E, tr = lg_ref.shape
    gsz = E // N_GROUPS
    ninf = -jnp.inf
    scores = jax.nn.sigmoid(lg_ref[...])
    choice = scores + bias_ref[...]

    def first_max(x, n):
        io = lax.broadcasted_iota(I32, x.shape, 0)
        m = jnp.max(x, axis=0, keepdims=True)
        return m, jnp.min(jnp.where(x == m, io, n), axis=0, keepdims=True), io

    gs_rows = []
    for g in range(N_GROUPS):
        cg = choice[g * gsz:(g + 1) * gsz, :]
        m1, i1, io = first_max(cg, gsz)
        m2 = jnp.max(jnp.where(io == i1, ninf, cg), axis=0, keepdims=True)
        gs_rows.append(m1 + m2)
    gs = jnp.concatenate(gs_rows, axis=0)
    gsel = jnp.zeros(gs.shape, F32)
    for _ in range(TOPK_GROUPS):
        _, gi, io = first_max(gs, N_GROUPS)
        hit = io == gi
        gsel = jnp.where(hit, 1.0, gsel)
        gs = jnp.where(hit, ninf, gs)
    gsel_e = jnp.concatenate(
        [jnp.broadcast_to(gsel[g:g + 1, :], (gsz, tr)) for g in range(N_GROUPS)], axis=0)
    masked = jnp.where(gsel_e > 0.0, choice, ninf)

    idx_rows, w_rows = [], []
    member = jnp.zeros((E, tr), F32)
    for _ in range(TOP_K):
        _, ii, io = first_max(masked, E)
        hit = io == ii
        idx_rows.append(ii)
        w_rows.append(jnp.sum(jnp.where(hit, scores, 0.0), axis=0, keepdims=True))
        member = jnp.where(hit, 1.0, member)
        masked = jnp.where(hit, ninf, masked)
    wsum = w_rows[0]
    for wk in w_rows[1:]:
        wsum = wsum + wk
    idx_ref[...] = jnp.concatenate(idx_rows, axis=0)
    w_ref[...] = jnp.concatenate([wk / wsum * ROUTED_SCALE for wk in w_rows], axis=0)

    before = jnp.dot(member.astype(BF16), tri_ref[...], preferred_element_type=F32) + carry_ref[...]
    io = lax.broadcasted_iota(I32, (E, tr), 0)
    rank_ref[...] = jnp.concatenate(
        [jnp.sum(jnp.where(io == ii, before, 0.0), axis=0, keepdims=True) for ii in idx_rows],
        axis=0).astype(I32)
    carry_ref[...] += jnp.sum(member, axis=1, keepdims=True)
    cnt_ref[...] = carry_ref[...]


def _route(logits_t, bias):
    E, T = logits_t.shape
    tr = _tile(T, 512)
    tri = jnp.asarray(np.triu(np.ones((tr, tr), np.float32), k=1), BF16)
    col = lambda i: (0, i)
    return pl.pallas_call(
        _route_kernel,
        grid=(T // tr,),
        in_specs=[pl.BlockSpec((E, tr), col), pl.BlockSpec((E, 1), lambda i: (0, 0)),
                  pl.BlockSpec((tr, tr), lambda i: (0, 0))],
        out_specs=[pl.BlockSpec((TOP_K, tr), col), pl.BlockSpec((TOP_K, tr), col),
                   pl.BlockSpec((TOP_K, tr), col), pl.BlockSpec((E, 1), lambda i: (0, 0))],
        out_shape=[jax.ShapeDtypeStruct((TOP_K, T), I32), jax.ShapeDtypeStruct((TOP_K, T), F32),
                   jax.ShapeDtypeStruct((TOP_K, T), I32), jax.ShapeDtypeStruct((E, 1), F32)],
        scratch_shapes=[pltpu.VMEM((E, 1), F32)],
        compiler_params=_params("arbitrary"),
        name="route",
    )(logits_t, bias, tri)


DISPATCH_TOKENS = SMEM_INDEX_CHUNK // TOP_K
PAD_SIZES = tuple(1 << s for s in reversed(range((EXPERT_BLOCK - 1).bit_length())))
assert PAD_SIZES[0] <= DISPATCH_TOKENS


def _dispatch_rows(step, tok0, pad_start_ref, pad_len_ref, pos_hbm, x_ref, xs_hbm,
                   idx_smem, idx_sem, row_sem, pad_sem):
    n = idx_smem.shape[0]
    base = pl.multiple_of(step * n, SMEM_INDEX_CHUNK)
    idx_cp = pltpu.make_async_copy(pos_hbm.at[pl.ds(base, n)], idx_smem, idx_sem)
    idx_cp.start()
    idx_cp.wait()

    def row_copy(t, dst_row):
        return pltpu.make_async_copy(x_ref.at[t], xs_hbm.at[dst_row], row_sem)

    def issue(t, c):
        for k in range(TOP_K):
            row_copy(tok0 + t, idx_smem[t * TOP_K + k]).start()
        return c

    def drain(r, c):
        row_copy(0, 0).wait()
        return c

    @pl.when(step == 0)
    def _():
        def pad_pass(act):
            def per_expert(e, c):
                ln = pad_len_ref[e]
                off = pad_start_ref[e]
                for sz in PAD_SIZES:
                    @pl.when((ln & sz) != 0)
                    def _():
                        dst = off + (ln & ~(2 * sz - 1))
                        act(pltpu.make_async_copy(x_ref.at[pl.ds(0, sz)], xs_hbm.at[pl.ds(dst, sz)],
                                                  pad_sem))
                return c
            lax.fori_loop(0, pad_len_ref.shape[0], per_expert, 0)

            run = PAD_SIZES[0]
            first = pad_start_ref[pad_len_ref.shape[0]]

            def per_run(q, c):
                act(pltpu.make_async_copy(x_ref.at[pl.ds(0, run)],
                                          xs_hbm.at[pl.ds(first + q * run, run)], pad_sem))
                return c
            lax.fori_loop(0, (xs_hbm.shape[0] - first) // run, per_run, 0)

        pad_pass(lambda cp: cp.start())
        pad_pass(lambda cp: cp.wait())

    return (lambda: lax.fori_loop(0, n // TOP_K, issue, 0, unroll=True),
            lambda: lax.fori_loop(0, n, drain, 0, unroll=True))


def _expert_kernel(be_ref, nxt_ref, nu_ref, xs_ref, w1_hbm, w3_hbm, w2_hbm, y_ref,
                   w1f, w3f, w2f, w1b, w3b, w2b, slot_ref, wsem):
    n = pl.program_id(0)
    used = n < nu_ref[0]

    @pl.when(jnp.logical_not(used))
    def _():
        y_ref[...] = jnp.zeros_like(y_ref)

    def weight_copies(e, s):
        return [pltpu.make_async_copy(src.at[e], dst.at[s], wsem.at[s])
                for src, dst in ((w1_hbm, w1f), (w3_hbm, w3f), (w2_hbm, w2f))]

    @pl.when(used)
    def _():
        prev = be_ref[jnp.maximum(n - 1, 0)]

        @pl.when((n == 0) | (be_ref[n] != prev))
        def _():
            @pl.when(n == 0)
            def _():
                slot_ref[0] = 1
                for cp in weight_copies(be_ref[0], 0):
                    cp.start()

            s = 1 - slot_ref[0]
            slot_ref[0] = s
            for cp in weight_copies(0, s):
                cp.wait()

            @pl.when(nxt_ref[n] >= 0)
            def _():
                for cp in weight_copies(nxt_ref[n], 1 - s):
                    cp.start(priority=1)

            w1b[...] = w1f[s].astype(BF16)
            w3b[...] = w3f[s].astype(BF16)
            w2b[...] = w2f[s].astype(BF16)

        lo, hi = _unpack_halves(_rows_to_tiles(xs_ref[...]))
        half = lo.shape[1]
        lo, hi = lo.astype(BF16), hi.astype(BF16)
        h1 = (jnp.dot(lo, w1b[:half, :], preferred_element_type=F32)
              + jnp.dot(hi, w1b[half:, :], preferred_element_type=F32))
        h3 = (jnp.dot(lo, w3b[:half, :], preferred_element_type=F32)
              + jnp.dot(hi, w3b[half:, :], preferred_element_type=F32))
        hdn = (jax.nn.silu(h1) * h3).astype(BF16)
        y = _pack_halves(jnp.dot(hdn, w2b[...], preferred_element_type=F32))
        y_ref[...] = _tiles_to_rows(y, y_ref.shape[1])


def _experts(block_e, next_e, n_used, xs, w1, w3, w2):
    P, S, L = xs.shape
    E, D, H = w1.shape
    nb = P // EXPERT_BLOCK
    grid_spec = pltpu.PrefetchScalarGridSpec(
        num_scalar_prefetch=3,
        grid=(nb,),
        in_specs=[pl.BlockSpec((EXPERT_BLOCK, S, L),
                               lambda n, be, nx, nu: (jnp.minimum(n, nu[0] - 1), 0, 0)),
                  pl.BlockSpec(memory_space=pl.ANY), pl.BlockSpec(memory_space=pl.ANY),
                  pl.BlockSpec(memory_space=pl.ANY)],
        out_specs=pl.BlockSpec((EXPERT_BLOCK, S, L), lambda n, be, nx, nu: (n, 0, 0)),
        scratch_shapes=[pltpu.VMEM((2, D, H), F32), pltpu.VMEM((2, D, H), F32), pltpu.VMEM((2, H, D), F32),
                        pltpu.VMEM((D, H), BF16), pltpu.VMEM((D, H), BF16), pltpu.VMEM((H, D), BF16),
                        pltpu.SMEM((1,), I32), pltpu.SemaphoreType.DMA((2,))],
    )
    return pl.pallas_call(
        _expert_kernel,
        grid_spec=grid_spec,
        out_shape=jax.ShapeDtypeStruct((P, S, L), U32),
        compiler_params=pltpu.CompilerParams(dimension_semantics=("arbitrary",),
                                             vmem_limit_bytes=EXPERT_VMEM_LIMIT),
        name="experts",
    )(block_e, next_e, n_used, xs, w1, w3, w2)


def _addend_kernel(pad_start_ref, pad_len_ref, pos_hbm, xu_ref, x_ref, p_ref, s1_ref, s3_ref, s2_ref,
                   wg_ref, wp_ref, o_ref, xs_hbm, h_ref, idx_smem, idx_sem, row_sem, pad_sem):
    j = pl.program_id(1)
    step = pl.program_id(0) * pl.num_programs(1) + j
    start_rows, wait_rows = _dispatch_rows(step, j * DISPATCH_TOKENS, pad_start_ref, pad_len_ref, pos_hbm,
                                           xu_ref, xs_hbm, idx_smem, idx_sem, row_sem, pad_sem)

    @pl.when(j == 0)
    def _():
        x = x_ref[...]
        h1 = jnp.dot(x, s1_ref[...], preferred_element_type=F32)
        h3 = jnp.dot(x, s3_ref[...], preferred_element_type=F32)
        h_ref[...] = (jax.nn.silu(h1) * h3).astype(BF16)

    start_rows()
    shared = jnp.dot(h_ref[...], s2_ref[...], preferred_element_type=F32)
    gate = jax.nn.sigmoid(jnp.dot(x_ref[...], wg_ref[...], preferred_element_type=F32))
    emb = jnp.dot(p_ref[...], wp_ref[...], preferred_element_type=F32)
    o_ref[...] = (shared + gate * emb).astype(o_ref.dtype)
    wait_rows()


def _addend_dispatch(pad_start, pad_len, pos_tok_major, x1u, n_rows, x1b, p2, s1, s3, s2, wg, wp):
    T, D = x1b.shape
    _, S, L = x1u.shape
    H = s1.shape[1]
    PD = p2.shape[1]
    tn = _tile(D, 1024)
    tm = DISPATCH_TOKENS * (D // tn)
    assert T % tm == 0
    grid_spec = pltpu.PrefetchScalarGridSpec(
        num_scalar_prefetch=2,
        grid=(T // tm, D // tn),
        in_specs=[pl.BlockSpec(memory_space=pl.ANY),
                  pl.BlockSpec((tm, S, L), lambda i, j, ps, pn: (i, 0, 0)),
                  pl.BlockSpec((tm, D), lambda i, j, ps, pn: (i, 0)),
                  pl.BlockSpec((tm, PD), lambda i, j, ps, pn: (i, 0)),
                  pl.BlockSpec((D, H), lambda i, j, ps, pn: (0, 0)),
                  pl.BlockSpec((D, H), lambda i, j, ps, pn: (0, 0)),
                  pl.BlockSpec((H, tn), lambda i, j, ps, pn: (0, j)),
                  pl.BlockSpec((D, tn), lambda i, j, ps, pn: (0, j)),
                  pl.BlockSpec((PD, tn), lambda i, j, ps, pn: (0, j))],
        out_specs=[pl.BlockSpec((tm, tn), lambda i, j, ps, pn: (i, j)),
                   pl.BlockSpec(memory_space=pl.ANY)],
        scratch_shapes=[pltpu.VMEM((tm, H), BF16), pltpu.SMEM((SMEM_INDEX_CHUNK,), I32),
                        pltpu.SemaphoreType.DMA(()), pltpu.SemaphoreType.DMA(()),
                        pltpu.SemaphoreType.DMA(())],
    )
    return pl.pallas_call(
        _addend_kernel,
        grid_spec=grid_spec,
        out_shape=[jax.ShapeDtypeStruct((T, D), BF16), jax.ShapeDtypeStruct((n_rows, S, L), x1u.dtype)],
        compiler_params=_params("arbitrary", "arbitrary"),
        name="shared_ple_dispatch",
    )(pad_start, pad_len, pos_tok_major, x1u, x1b, p2, s1, s3, s2, wg, wp)


def _final_kernel(pos_hbm, y_hbm, x1_ref, add_ref, w_ref, g_ref, b_ref, o_ref,
                  buf_a, buf_b, idx_smem, idx_sem, row_sems, *, alpha):
    i = pl.program_id(0)
    last = pl.num_programs(0) - 1
    tm = x1_ref.shape[0]
    n = tm * TOP_K
    bufs = (buf_a, buf_b)

    def row_copy(slot, r, src_row):
        return pltpu.make_async_copy(y_hbm.at[src_row], bufs[slot].at[r], row_sems.at[slot])

    def fetch(step, slot, unroll):
        base = pl.multiple_of(step * n, SMEM_INDEX_CHUNK)
        idx_cp = pltpu.make_async_copy(pos_hbm.at[pl.ds(base, n)], idx_smem, idx_sem)
        idx_cp.start()
        idx_cp.wait()

        def issue(r, c):
            row_copy(slot, r, idx_smem[r]).start()
            return c

        lax.fori_loop(0, n, issue, 0, unroll=unroll)

    def drain(slot):
        def one(r, c):
            row_copy(slot, 0, 0).wait()
            return c
        lax.fori_loop(0, n, one, 0, unroll=True)

    @pl.when(i == 0)
    def _():
        fetch(0, 0, DMA_LOOP_UNROLL)

    def step(slot):
        drain(slot)
        fetch(jnp.minimum(i + 1, last), 1 - slot, True)

        buf = bufs[slot]
        acc_lo = jnp.zeros((tm,) + buf.shape[1:], F32)
        acc_hi = acc_lo
        for k in range(TOP_K):
            lo, hi = _unpack_halves(buf[k * tm:(k + 1) * tm])
            wk = w_ref[:, k:k + 1, :]
            acc_lo += wk * lo
            acc_hi += wk * hi
        routed = jnp.concatenate([_rows_to_tiles(acc_lo), _rows_to_tiles(acc_hi)], axis=1)
        pre = alpha * x1_ref[...] + add_ref[...].astype(F32) + routed
        mu = jnp.mean(pre, axis=-1, keepdims=True)
        xc = pre - mu
        var = jnp.mean(xc * xc, axis=-1, keepdims=True)
        o_ref[...] = xc * lax.rsqrt(var + LN_EPS) * g_ref[...] + b_ref[...]

        @pl.when(i == last)
        def _():
            drain(1 - slot)

    for slot in range(2):
        pl.when(i % 2 == slot)(functools.partial(step, slot))


def _final(pos_flat, y_sorted, x1, addend, w_tok, g, b, alpha):
    T, D = x1.shape
    _, S, L = y_sorted.shape
    tm = SMEM_INDEX_CHUNK // TOP_K
    row = lambda i: (i, 0)
    const = lambda i: (0, 0)
    return pl.pallas_call(
        functools.partial(_final_kernel, alpha=alpha),
        grid=(T // tm,),
        in_specs=[pl.BlockSpec(memory_space=pl.ANY), pl.BlockSpec(memory_space=pl.ANY),
                  pl.BlockSpec((tm, D), row), pl.BlockSpec((tm, D), row),
                  pl.BlockSpec((tm, TOP_K, LANES), lambda i: (i, 0, 0)),
                  pl.BlockSpec((1, D), const), pl.BlockSpec((1, D), const)],
        out_specs=pl.BlockSpec((tm, D), row),
        out_shape=jax.ShapeDtypeStruct((T, D), F32),
        scratch_shapes=[pltpu.VMEM((tm * TOP_K, S, L), U32), pltpu.VMEM((tm * TOP_K, S, L), U32),
                        pltpu.SMEM((tm * TOP_K,), I32),
                        pltpu.SemaphoreType.DMA(()), pltpu.SemaphoreType.DMA((2,))],
        compiler_params=_params("arbitrary"),
        name="combine_ln2",
    )(pos_flat, y_sorted, x1, addend, w_tok, g, b)


def _dispatch_plan(top_idx, rank, counts, T):
    E = counts.shape[0]
    blk = EXPERT_BLOCK
    n_blocks = -(-(T * TOP_K) // blk) + E
    padded = (counts + blk - 1) // blk * blk
    ends = jnp.cumsum(padded)
    starts = ends - padded
    onehot = top_idx[None, :, :] == jnp.arange(E, dtype=I32)[:, None, None]
    pos = jnp.sum(jnp.where(onehot, starts[:, None, None], 0), axis=0) + rank
    first_row = jnp.arange(n_blocks, dtype=I32) * blk
    block_e = jnp.minimum(jnp.sum(ends[None, :] <= first_row[:, None], axis=1), E - 1).astype(I32)
    n_used = (ends[-1:] // blk).astype(I32)
    bidx = jnp.arange(n_blocks, dtype=I32)
    run_start = (bidx > 0) & (bidx < n_used[0]) & (block_e != jnp.roll(block_e, 1))
    nxt_start = lax.cummin(jnp.where(run_start, bidx, n_blocks), reverse=True)
    nxt_start = jnp.concatenate([nxt_start[1:], jnp.full((1,), n_blocks, I32)])
    next_e = jnp.where(nxt_start < n_blocks, block_e[jnp.minimum(nxt_start, n_blocks - 1)], -1)
    pad_start = jnp.concatenate([starts + counts, ends[-1:]])
    return pos, pad_start, padded - counts, block_e, next_e.astype(I32), n_used, n_blocks * blk


def kernel(x, p, w_in, pool_w, pool_scale, lb_param, hg_norm_w, w_branch_a, w_branch_b, w_out, ln1_g, ln1_b, router_w, router_bias, exp_w1, exp_w3, exp_w2, sh_w1, sh_w3, sh_w2, ple_gate_w, ple_proj_w, ln2_g, ln2_b):
    B, S, D = x.shape
    T = B * S
    depth = w_in.shape[0]
    alpha = (2.0 * depth) ** 0.25
    pool_width = w_branch_a.shape[1]
    hg_width = w_branch_b.shape[1]
    col_q = pool_width
    col_ga = pool_width + 4 * hg_width
    col_gb = col_ga + D
    lb_all = jnp.cumsum(jax.nn.softmax(lb_param.astype(F32), axis=0), axis=0)

    for i in range(depth):
        x2 = x.reshape(T, D)
        proj = _matmul(x2.astype(BF16), w_in[i].astype(BF16), BF16, 1024, 1024)
        o_b = _hgrn(proj.reshape(B, S, -1), lb_all[i][None, :], hg_norm_w[i][None, :].astype(F32),
                    col_q, hg_width).reshape(T, hg_width)
        mixed = _mix(proj, o_b, pool_w[i].astype(BF16), pool_scale[i][None, :].astype(F32),
                     w_branch_a[i].astype(BF16), w_branch_b[i].astype(BF16), S, col_ga, col_gb)
        pre = _outproj(mixed, w_out[i].astype(BF16), x2, alpha)

        rw_t = router_w[i].astype(F32).T
        rw_hi = rw_t.astype(BF16)
        rw_lo = (rw_t - rw_hi.astype(F32)).astype(BF16)
        x1, x1b, x1u, logits_t = _ln1(pre, ln1_g[i][None, :], ln1_b[i][None, :], rw_hi, rw_lo)

        top_idx, gate_w, rank, counts = _route(logits_t, router_bias[i].astype(F32)[:, None])
        pos, pad_start, pad_len, block_e, next_e, n_used, n_rows = _dispatch_plan(
            top_idx, rank, counts[:, 0].astype(I32), T)
        addend, xs = _addend_dispatch(
            pad_start, pad_len, pos.T.reshape(-1), x1u, n_rows,
            x1b, p[i].reshape(T, -1).astype(BF16), sh_w1[i].astype(BF16), sh_w3[i].astype(BF16),
            sh_w2[i].astype(BF16), ple_gate_w[i].astype(BF16), ple_proj_w[i].astype(BF16))
        ys = _experts(block_e, next_e, n_used, xs, exp_w1[i].astype(F32), exp_w3[i].astype(F32),
                      exp_w2[i].astype(F32))

        tm = SMEM_INDEX_CHUNK // TOP_K
        pos_flat = pos.reshape(TOP_K, T // tm, tm).transpose(1, 0, 2).reshape(-1)
        w_rep = jnp.broadcast_to(gate_w.T[:, :, None], (T, TOP_K, LANES))
        out = _final(pos_flat, ys, x1, addend, w_rep, ln2_g[i][None, :], ln2_b[i][None, :], alpha)
        x = out.reshape(B, S, D)
    return x
```

```python
import functools

import numpy as np
import jax
import jax.numpy as jnp
from jax import lax
from jax.experimental import pallas as pl
from jax.experimental.pallas import tpu as pltpu

F32 = jnp.float32
BF16 = jnp.bfloat16
U32 = jnp.uint32
I32 = jnp.int32

V7X_VMEM_BYTES = 64 * 1024 * 1024
VMEM_LIMIT = V7X_VMEM_BYTES - 12 * 1024 * 1024
EXPERT_VMEM_LIMIT = V7X_VMEM_BYTES - 6 * 1024 * 1024
LANES = 128
SMEM_INDEX_CHUNK = 1024

POOL_WINDOWS = (2, 4, 8, 16)
HG_DIM = 128
N_GROUPS = 8
TOPK_GROUPS = 4
TOP_K = 8
ROUTED_SCALE = 2.5
LN_EPS = 1e-5
RMS_EPS = 1e-6

HG_CHUNK = 64
HG_HEADS_PER_STEP = 8
EXPERT_BLOCK = 256
POOL_HALO = 16
DMA_LOOP_UNROLL = 8
DMA_QUEUES = 2


def _tile(n, pref):
    t = min(n, pref)
    while n % t:
        t -= 1
    return t


def _params(*sem):
    return pltpu.CompilerParams(dimension_semantics=sem, vmem_limit_bytes=VMEM_LIMIT)


def _mm_kernel(a_ref, b_ref, o_ref):
    o_ref[...] = jnp.dot(a_ref[...], b_ref[...], preferred_element_type=F32).astype(o_ref.dtype)


def _matmul(a, b, out_dtype, tm, tn):
    M, K = a.shape
    N = b.shape[1]
    tm, tn = _tile(M, tm), _tile(N, tn)
    return pl.pallas_call(
        _mm_kernel,
        grid=(M // tm, N // tn),
        in_specs=[pl.BlockSpec((tm, K), lambda i, j: (i, 0)),
                  pl.BlockSpec((K, tn), lambda i, j: (0, j))],
        out_specs=pl.BlockSpec((tm, tn), lambda i, j: (i, j)),
        out_shape=jax.ShapeDtypeStruct((M, N), out_dtype),
        compiler_params=_params("parallel", "parallel"),
        name="in_proj",
    )(a, b)


def _hgrn_consts():
    C = HG_CHUNK
    t = np.arange(C)[:, None]
    u = np.arange(C)[None, :]
    blocks = [u <= t, u > t]
    masks = []
    h = C // 2
    while h >= 1:
        same = (t // (2 * h)) == (u // (2 * h))
        t2 = (t % (2 * h)) >= h
        u2 = (u % (2 * h)) >= h
        blocks.append(same & ((t2 & u2 & (u <= t)) | (~t2 & ~u2 & (u > t))))
        masks.append(same & t2 & ~u2)
        h //= 2
    masks.append(t == u)
    wall = np.concatenate(blocks, axis=0).astype(np.float32)
    mask = np.stack(masks).astype(np.float32)
    return jnp.asarray(wall, BF16), jnp.asarray(mask, F32)


def _hgrn_kernel(q_ref, f_ref, i_ref, og_ref, lb_ref, nw_ref, wall_ref, mask_ref,
                 o_ref, *st_refs, heads, nchunk, nlev):
    C = HG_CHUNK
    D = HG_DIM

    @pl.when(pl.program_id(2) == 0)
    def _():
        for st_ref in st_refs:
            st_ref[...] = jnp.zeros_like(st_ref)

    nt = (((1,), (1,)), ((), ()))
    tn = (((0,), (0,)), ((), ()))

    def chunk(c, carry):
        r0 = pl.multiple_of(c * C, C)
        rows = pl.ds(r0, C)
        hs = range(heads)
        cols = [slice(j * D, (j + 1) * D) for j in hs]
        qs, ks, vs, gcat = [], [], [], []
        for cs in cols:
            lb = lb_ref[:, cs]
            fl = f_ref[rows, cs].astype(F32)
            e = jnp.exp(-jnp.abs(fl))
            r = 1.0 / (1.0 + e)
            pos = fl >= 0
            sig_p = jnp.where(pos, r, e * r)
            sig_n = jnp.where(pos, e * r, r)
            g = jnp.log(lb + (1.0 - lb) * sig_p)
            ks.append((1.0 - lb) * sig_n)
            qs.append(jax.nn.silu(q_ref[rows, cs].astype(F32)))
            vs.append(i_ref[rows, cs])
            g_hi = g.astype(BF16)
            gcat += [g_hi, (g - g_hi.astype(F32)).astype(BF16)]
        e2 = jnp.dot(wall_ref[...], jnp.concatenate(gcat, axis=1), preferred_element_type=F32)
        exs = [e2[:, 2 * j * D:(2 * j + 1) * D] + e2[:, (2 * j + 1) * D:(2 * j + 2) * D] for j in hs]
        sts = [st_refs[j][...] for j in hs]
        o_inter = [lax.dot_general((qs[j] * jnp.exp(exs[j][0:C])).astype(BF16), sts[j].astype(BF16), nt,
                                   preferred_element_type=F32) for j in hs]
        a = [lax.dot_general(qs[j].astype(BF16), ks[j].astype(BF16), nt,
                             preferred_element_type=F32) * mask_ref[nlev] for j in hs]
        for l in range(nlev):
            for j in hs:
                xl = jnp.exp(exs[j][(2 + l) * C:(3 + l) * C])
                a[j] = a[j] + lax.dot_general((qs[j] * xl).astype(BF16), (ks[j] * xl).astype(BF16), nt,
                                              preferred_element_type=F32) * mask_ref[l]
        os_ = [o_inter[j] + jnp.dot(a[j].astype(BF16), vs[j], preferred_element_type=F32) for j in hs]
        for j in hs:
            k_dec = (ks[j] * jnp.exp(exs[j][C:2 * C])).astype(BF16)
            st_refs[j][...] = sts[j] * jnp.exp(exs[j][C - 1:C, :]) + lax.dot_general(
                vs[j], k_dec, tn, preferred_element_type=F32)
        for j, cs in enumerate(cols):
            o = os_[j]
            o = o * lax.rsqrt(jnp.mean(o * o, axis=-1, keepdims=True) + RMS_EPS)
            o = o * nw_ref[:, cs] * jax.nn.silu(og_ref[rows, cs].astype(F32))
            o_ref[rows, cs] = o.astype(o_ref.dtype)
        return carry

    lax.fori_loop(0, nchunk, chunk, 0)


def _hgrn(proj3, lb, norm_w, col_q, width):
    B, S, _ = proj3.shape
    hw = HG_DIM * HG_HEADS_PER_STEP
    ts = _tile(S, 1024)
    wall, mask = _hgrn_consts()
    nlev = mask.shape[0] - 1
    off = [(col_q + k * width) // hw for k in range(4)]

    def seg(k):
        return pl.BlockSpec((None, ts, hw), lambda b, h, s: (b, s, off[k] + h))

    kern = functools.partial(_hgrn_kernel, heads=HG_HEADS_PER_STEP, nchunk=ts // HG_CHUNK, nlev=nlev)
    return pl.pallas_call(
        kern,
        grid=(B, width // hw, S // ts),
        in_specs=[seg(0), seg(1), seg(2), seg(3),
                  pl.BlockSpec((1, hw), lambda b, h, s: (0, h)),
                  pl.BlockSpec((1, hw), lambda b, h, s: (0, h)),
                  pl.BlockSpec(wall.shape, lambda b, h, s: (0, 0)),
                  pl.BlockSpec(mask.shape, lambda b, h, s: (0, 0, 0))],
        out_specs=pl.BlockSpec((None, ts, hw), lambda b, h, s: (b, s, h)),
        out_shape=jax.ShapeDtypeStruct((B, S, width), BF16),
        scratch_shapes=[pltpu.VMEM((HG_DIM, HG_DIM), F32)] * HG_HEADS_PER_STEP,
        compiler_params=_params("parallel", "parallel", "arbitrary"),
        name="hgrn2",
    )(proj3, proj3, proj3, proj3, lb, norm_w, wall, mask)


def _mix_kernel(u_ref, o_ref, ga_ref, gb_ref, pw_ref, ps_ref, wa_ref, wb_ref,
                out_ref, pooled_ref, halo_ref, *, tiles_per_seq):
    i = pl.program_id(0)
    tm = u_ref.shape[0]
    gw = pw_ref.shape[1]

    @pl.when(pl.program_id(1) == 0)
    def _():
        @pl.when(i % tiles_per_seq == 0)
        def _():
            halo_ref[...] = jnp.zeros_like(halo_ref)

        u = u_ref[...].astype(F32)
        ext = jnp.concatenate([halo_ref[...], u], axis=0)
        halo_ref[...] = u[tm - POOL_HALO:, :]
        t1 = ((i % tiles_per_seq) * tm + 1 + lax.broadcasted_iota(I32, (tm, 1), 0)).astype(F32)
        for g, w in enumerate(POOL_WINDOWS):
            cs = slice(g * gw, (g + 1) * gw)
            s = ext[:, cs]
            shift = 1
            while shift < w:
                s = s + pltpu.roll(s, shift, 0)
                shift *= 2
            d = s[POOL_HALO:, :] * (1.0 / jnp.minimum(t1, float(w))) - u[:, cs]
            y = jnp.dot(d.astype(BF16), pw_ref[g], preferred_element_type=F32) * ps_ref[:, cs]
            pooled_ref[:, cs] = y.astype(BF16)

    ya = jnp.dot(pooled_ref[...], wa_ref[...], preferred_element_type=F32)
    yb = jnp.dot(o_ref[...], wb_ref[...], preferred_element_type=F32)
    mixed = (jax.nn.sigmoid(ga_ref[...].astype(F32)) * ya
             + jax.nn.sigmoid(gb_ref[...].astype(F32)) * yb)
    out_ref[...] = mixed.astype(out_ref.dtype)


def _mix(proj, o_b, pool_w, pool_scale, wa, wb, seq, col_ga, col_gb):
    T = proj.shape[0]
    pwid = wa.shape[0]
    D = wa.shape[1]
    tm = _tile(seq, 512)
    tn = _tile(D, 1024)
    kern = functools.partial(_mix_kernel, tiles_per_seq=seq // tm)
    return pl.pallas_call(
        kern,
        grid=(T // tm, D // tn),
        in_specs=[pl.BlockSpec((tm, pwid), lambda i, j: (i, 0)),
                  pl.BlockSpec((tm, o_b.shape[1]), lambda i, j: (i, 0)),
                  pl.BlockSpec((tm, tn), lambda i, j: (i, col_ga // tn + j)),
                  pl.BlockSpec((tm, tn), lambda i, j: (i, col_gb // tn + j)),
                  pl.BlockSpec(pool_w.shape, lambda i, j: (0, 0, 0)),
                  pl.BlockSpec((1, pwid), lambda i, j: (0, 0)),
                  pl.BlockSpec((pwid, tn), lambda i, j: (0, j)),
                  pl.BlockSpec((wb.shape[0], tn), lambda i, j: (0, j))],
        out_specs=pl.BlockSpec((tm, tn), lambda i, j: (i, j)),
        out_shape=jax.ShapeDtypeStruct((T, D), BF16),
        scratch_shapes=[pltpu.VMEM((tm, pwid), BF16), pltpu.VMEM((POOL_HALO, pwid), F32)],
        compiler_params=_params("arbitrary", "arbitrary"),
        name="pool_mix",
    )(proj, o_b, proj, proj, pool_w, pool_scale, wa, wb)


def _outproj_kernel(a_ref, w_ref, x_ref, o_ref, *, alpha):
    o_ref[...] = alpha * x_ref[...] + jnp.dot(a_ref[...], w_ref[...], preferred_element_type=F32)


def _outproj(mixed, w_out, x2, alpha):
    T, D = x2.shape
    tm, tn = _tile(T, 512), _tile(D, 1024)
    return pl.pallas_call(
        functools.partial(_outproj_kernel, alpha=alpha),
        grid=(T // tm, D // tn),
        in_specs=[pl.BlockSpec((tm, D), lambda i, j: (i, 0)),
                  pl.BlockSpec((D, tn), lambda i, j: (0, j)),
                  pl.BlockSpec((tm, tn), lambda i, j: (i, j))],
        out_specs=pl.BlockSpec((tm, tn), lambda i, j: (i, j)),
        out_shape=jax.ShapeDtypeStruct((T, D), F32),
        compiler_params=_params("parallel", "parallel"),
        name="out_proj",
    )(mixed, w_out, x2)


def _pack_halves(x):
    n = x.shape[1] // 2
    lo = pltpu.bitcast(x[:, :n].astype(BF16).astype(F32), U32)
    hi = pltpu.bitcast(x[:, n:].astype(BF16).astype(F32), U32)
    return (lo >> 16) | (hi & jnp.uint32(0xFFFF0000))


def _unpack_halves(u):
    lo = pltpu.bitcast(u << 16, F32)
    hi = pltpu.bitcast(u & jnp.uint32(0xFFFF0000), F32)
    return lo, hi


SUBLANES = 8


def _exchange(a, inner):
    n = a.shape[0]
    s_idx = lax.broadcasted_iota(I32, (1, 1, SUBLANES, LANES), 2)
    d = SUBLANES // 2
    while d >= 1:
        span = d * inner
        a4 = a.reshape(n // (2 * span), 2, span, SUBLANES, LANES)
        lo, hi = a4[:, 0], a4[:, 1]
        keep = (s_idx & d) == 0
        new_lo = jnp.where(keep, lo, pltpu.roll(hi, d, 2))
        new_hi = jnp.where(keep, pltpu.roll(lo, SUBLANES - d, 2), hi)
        a = jnp.stack([new_lo, new_hi], axis=1).reshape(n, SUBLANES, LANES)
        d //= 2
    return a


def _rows_to_tiles(x3):
    r, c, _ = x3.shape
    inner = c // SUBLANES
    a = _exchange(x3.reshape(r * inner, SUBLANES, LANES), inner)
    a = a.reshape(r // SUBLANES, SUBLANES, inner, SUBLANES, LANES)
    return jnp.concatenate([a[:, ci % SUBLANES, ci // SUBLANES].reshape(r, LANES) for ci in range(c)],
                           axis=1)


def _tiles_to_rows(x, c):
    r = x.shape[0]
    inner = c // SUBLANES
    g = r // SUBLANES
    cols = [x[:, ci * LANES:(ci + 1) * LANES].reshape(g, 1, 1, SUBLANES, LANES) for ci in range(c)]
    a = jnp.concatenate(
        [jnp.concatenate([cols[hi * SUBLANES + lo] for hi in range(inner)], axis=2)
         for lo in range(SUBLANES)], axis=1)
    a = _exchange(a.reshape(r * inner, SUBLANES, LANES), inner)
    return a.reshape(r, c, LANES)


def _ln1_kernel(pre_ref, g_ref, b_ref, rwh_ref, rwl_ref, x1_ref, x1b_ref, x1u_ref, lg_ref):
    x = pre_ref[...]
    mu = jnp.mean(x, axis=-1, keepdims=True)
    xc = x - mu
    var = jnp.mean(xc * xc, axis=-1, keepdims=True)
    y = xc * lax.rsqrt(var + LN_EPS) * g_ref[...] + b_ref[...]
    x1_ref[...] = y
    y_hi = y.astype(BF16)
    x1b_ref[...] = y_hi
    x1u_ref[...] = _tiles_to_rows(_pack_halves(y), x1u_ref.shape[1])
    y_lo = (y - y_hi.astype(F32)).astype(BF16)
    nt = (((1,), (1,)), ((), ()))
    lg = lax.dot_general(rwh_ref[...], y_hi, nt, preferred_element_type=F32)
    lg += lax.dot_general(rwh_ref[...], y_lo, nt, preferred_element_type=F32)
    lg += lax.dot_general(rwl_ref[...], y_hi, nt, preferred_element_type=F32)
    lg_ref[...] = lg


def _ln1(pre, g, b, rw_hi_t, rw_lo_t):
    T, D = pre.shape
    E = rw_hi_t.shape[0]
    tm = _tile(T, 256)
    row = lambda i: (i, 0)
    const = lambda i: (0, 0)
    return pl.pallas_call(
        _ln1_kernel,
        grid=(T // tm,),
        in_specs=[pl.BlockSpec((tm, D), row), pl.BlockSpec((1, D), const), pl.BlockSpec((1, D), const),
                  pl.BlockSpec((E, D), const), pl.BlockSpec((E, D), const)],
        out_specs=[pl.BlockSpec((tm, D), row), pl.BlockSpec((tm, D), row),
                   pl.BlockSpec((tm, D // 2 // LANES, LANES), lambda i: (i, 0, 0)),
                   pl.BlockSpec((E, tm), lambda i: (0, i))],
        out_shape=[jax.ShapeDtypeStruct((T, D), F32), jax.ShapeDtypeStruct((T, D), BF16),
                   jax.ShapeDtypeStruct((T, D // 2 // LANES, LANES), U32),
                   jax.ShapeDtypeStruct((E, T), F32)],
        compiler_params=_params("parallel"),
        name="ln1_router",
    )(pre, g, b, rw_hi_t, rw_lo_t)


def _route_kernel(lg_ref, bias_ref, tri_ref, idx_ref, w_ref, rank_ref, cnt_ref, carry_ref):
    @pl.when(pl.program_id(0) == 0)
    def _():
        carry_ref[...] = jnp.zeros_like(carry_ref)

    E, tr = lg_ref.shape
    gsz = E // N_GROUPS
    ninf = -jnp.inf
    scores = jax.nn.sigmoid(lg_ref[...])
    choice = scores + bias_ref[...]

    def first_max(x, n):
        io = lax.broadcasted_iota(I32, x.shape, 0)
        m = jnp.max(x, axis=0, keepdims=True)
        return m, jnp.min(jnp.where(x == m, io, n), axis=0, keepdims=True), io

    gs_rows = []
    for g in range(N_GROUPS):
        cg = choice[g * gsz:(g + 1) * gsz, :]
        m1, i1, io = first_max(cg, gsz)
        m2 = jnp.max(jnp.where(io == i1, ninf, cg), axis=0, keepdims=True)
        gs_rows.append(m1 + m2)
    gs = jnp.concatenate(gs_rows, axis=0)
    gsel = jnp.zeros(gs.shape, F32)
    for _ in range(TOPK_GROUPS):
        _, gi, io = first_max(gs, N_GROUPS)
        hit = io == gi
        gsel = jnp.where(hit, 1.0, gsel)
        gs = jnp.where(hit, ninf, gs)
    gsel_e = jnp.concatenate(
        [jnp.broadcast_to(gsel[g:g + 1, :], (gsz, tr)) for g in range(N_GROUPS)], axis=0)
    masked = jnp.where(gsel_e > 0.0, choice, ninf)

    idx_rows, w_rows = [], []
    member = jnp.zeros((E, tr), F32)
    for _ in range(TOP_K):
        _, ii, io = first_max(masked, E)
        hit = io == ii
        idx_rows.append(ii)
        w_rows.append(jnp.sum(jnp.where(hit, scores, 0.0), axis=0, keepdims=True))
        member = jnp.where(hit, 1.0, member)
        masked = jnp.where(hit, ninf, masked)
    wsum = w_rows[0]
    for wk in w_rows[1:]:
        wsum = wsum + wk
    idx_ref[...] = jnp.concatenate(idx_rows, axis=0)
    w_ref[...] = jnp.concatenate([wk / wsum * ROUTED_SCALE for wk in w_rows], axis=0)

    before = jnp.dot(member.astype(BF16), tri_ref[...], preferred_element_type=F32) + carry_ref[...]
    io = lax.broadcasted_iota(I32, (E, tr), 0)
    rank_ref[...] = jnp.concatenate(
        [jnp.sum(jnp.where(io == ii, before, 0.0), axis=0, keepdims=True) for ii in idx_rows],
        axis=0).astype(I32)
    carry_ref[...] += jnp.sum(member, axis=1, keepdims=True)
    cnt_ref[...] = carry_ref[...]


def _route(logits_t, bias):
    E, T = logits_t.shape
    tr = _tile(T, 512)
    tri = jnp.asarray(np.triu(np.ones((tr, tr), np.float32), k=1), BF16)
    col = lambda i: (0, i)
    return pl.pallas_call(
        _route_kernel,
        grid=(T // tr,),
        in_specs=[pl.BlockSpec((E, tr), col), pl.BlockSpec((E, 1), lambda i: (0, 0)),
                  pl.BlockSpec((tr, tr), lambda i: (0, 0))],
        out_specs=[pl.BlockSpec((TOP_K, tr), col), pl.BlockSpec((TOP_K, tr), col),
                   pl.BlockSpec((TOP_K, tr), col), pl.BlockSpec((E, 1), lambda i: (0, 0))],
        out_shape=[jax.ShapeDtypeStruct((TOP_K, T), I32), jax.ShapeDtypeStruct((TOP_K, T), F32),
                   jax.ShapeDtypeStruct((TOP_K, T), I32), jax.ShapeDtypeStruct((E, 1), F32)],
        scratch_shapes=[pltpu.VMEM((E, 1), F32)],
        compiler_params=_params("arbitrary"),
        name="route",
    )(logits_t, bias, tri)


DISPATCH_TOKENS = SMEM_INDEX_CHUNK // TOP_K
PAD_SIZES = tuple(1 << s for s in reversed(range((EXPERT_BLOCK - 1).bit_length())))
assert PAD_SIZES[0] <= DISPATCH_TOKENS


def _dispatch_rows(step, tok0, pad_start_ref, pad_len_ref, pos_hbm, x_ref, xs_hbm,
                   idx_smem, idx_sem, row_sem, pad_sem):
    n = idx_smem.shape[0]
    base = pl.multiple_of(step * n, SMEM_INDEX_CHUNK)
    idx_cp = pltpu.make_async_copy(pos_hbm.at[pl.ds(base, n)], idx_smem, idx_sem)
    idx_cp.start()
    idx_cp.wait()

    def row_copy(t, dst_row):
        return pltpu.make_async_copy(x_ref.at[t], xs_hbm.at[dst_row], row_sem)

    def issue(t, c):
        for k in range(TOP_K):
            row_copy(tok0 + t, idx_smem[t * TOP_K + k]).start(priority=k % DMA_QUEUES)
        return c

    def drain(r, c):
        row_copy(0, 0).wait()
        return c

    @pl.when(step == 0)
    def _():
        def pad_pass(act):
            def per_expert(e, c):
                ln = pad_len_ref[e]
                off = pad_start_ref[e]
                for sz in PAD_SIZES:
                    @pl.when((ln & sz) != 0)
                    def _():
                        dst = off + (ln & ~(2 * sz - 1))
                        act(pltpu.make_async_copy(x_ref.at[pl.ds(0, sz)], xs_hbm.at[pl.ds(dst, sz)],
                                                  pad_sem))
                return c
            lax.fori_loop(0, pad_len_ref.shape[0], per_expert, 0)

            run = PAD_SIZES[0]
            first = pad_start_ref[pad_len_ref.shape[0]]

            def per_run(q, c):
                act(pltpu.make_async_copy(x_ref.at[pl.ds(0, run)],
                                          xs_hbm.at[pl.ds(first + q * run, run)], pad_sem))
                return c
            lax.fori_loop(0, (xs_hbm.shape[0] - first) // run, per_run, 0)

        pad_pass(lambda cp: cp.start())
        pad_pass(lambda cp: cp.wait())

    return (lambda: lax.fori_loop(0, n // TOP_K, issue, 0),
            lambda: lax.fori_loop(0, n, drain, 0, unroll=True))


def _expert_kernel(be_ref, nxt_ref, nu_ref, xs_ref, w1_hbm, w3_hbm, w2_hbm, y_ref,
                   w1f, w3f, w2f, w1b, w3b, w2b, slot_ref, wsem):
    n = pl.program_id(0)
    used = n < nu_ref[0]

    @pl.when(jnp.logical_not(used))
    def _():
        y_ref[...] = jnp.zeros_like(y_ref)

    def weight_copies(e, s):
        return [pltpu.make_async_copy(src.at[e], dst.at[s], wsem.at[s])
                for src, dst in ((w1_hbm, w1f), (w3_hbm, w3f), (w2_hbm, w2f))]

    @pl.when(used)
    def _():
        prev = be_ref[jnp.maximum(n - 1, 0)]

        @pl.when((n == 0) | (be_ref[n] != prev))
        def _():
            @pl.when(n == 0)
            def _():
                slot_ref[0] = 1
                for cp in weight_copies(be_ref[0], 0):
                    cp.start()

            s = 1 - slot_ref[0]
            slot_ref[0] = s
            for cp in weight_copies(0, s):
                cp.wait()

            @pl.when(nxt_ref[n] >= 0)
            def _():
                for cp in weight_copies(nxt_ref[n], 1 - s):
                    cp.start(priority=1)

            w1b[...] = w1f[s].astype(BF16)
            w3b[...] = w3f[s].astype(BF16)
            w2b[...] = w2f[s].astype(BF16)

        lo, hi = _unpack_halves(_rows_to_tiles(xs_ref[...]))
        half = lo.shape[1]
        lo, hi = lo.astype(BF16), hi.astype(BF16)
        h1 = (jnp.dot(lo, w1b[:half, :], preferred_element_type=F32)
              + jnp.dot(hi, w1b[half:, :], preferred_element_type=F32))
        h3 = (jnp.dot(lo, w3b[:half, :], preferred_element_type=F32)
              + jnp.dot(hi, w3b[half:, :], preferred_element_type=F32))
        hdn = (jax.nn.silu(h1) * h3).astype(BF16)
        y = _pack_halves(jnp.dot(hdn, w2b[...], preferred_element_type=F32))
        y_ref[...] = _tiles_to_rows(y, y_ref.shape[1])


def _experts(block_e, next_e, n_used, xs, w1, w3, w2):
    P, S, L = xs.shape
    E, D, H = w1.shape
    nb = P // EXPERT_BLOCK
    grid_spec = pltpu.PrefetchScalarGridSpec(
        num_scalar_prefetch=3,
        grid=(nb,),
        in_specs=[pl.BlockSpec((EXPERT_BLOCK, S, L),
                               lambda n, be, nx, nu: (jnp.minimum(n, nu[0] - 1), 0, 0)),
                  pl.BlockSpec(memory_space=pl.ANY), pl.BlockSpec(memory_space=pl.ANY),
                  pl.BlockSpec(memory_space=pl.ANY)],
        out_specs=pl.BlockSpec((EXPERT_BLOCK, S, L), lambda n, be, nx, nu: (n, 0, 0)),
        scratch_shapes=[pltpu.VMEM((2, D, H), F32), pltpu.VMEM((2, D, H), F32), pltpu.VMEM((2, H, D), F32),
                        pltpu.VMEM((D, H), BF16), pltpu.VMEM((D, H), BF16), pltpu.VMEM((H, D), BF16),
                        pltpu.SMEM((1,), I32), pltpu.SemaphoreType.DMA((2,))],
    )
    return pl.pallas_call(
        _expert_kernel,
        grid_spec=grid_spec,
        out_shape=jax.ShapeDtypeStruct((P, S, L), U32),
        compiler_params=pltpu.CompilerParams(dimension_semantics=("arbitrary",),
                                             vmem_limit_bytes=EXPERT_VMEM_LIMIT),
        name="experts",
    )(block_e, next_e, n_used, xs, w1, w3, w2)


def _addend_kernel(pad_start_ref, pad_len_ref, pos_hbm, xu_ref, x_ref, p_ref, s1_ref, s3_ref, s2_ref,
                   wg_ref, wp_ref, o_ref, xs_hbm, h_ref, idx_smem, idx_sem, row_sem, pad_sem):
    j = pl.program_id(1)
    step = pl.program_id(0) * pl.num_programs(1) + j
    start_rows, wait_rows = _dispatch_rows(step, j * DISPATCH_TOKENS, pad_start_ref, pad_len_ref, pos_hbm,
                                           xu_ref, xs_hbm, idx_smem, idx_sem, row_sem, pad_sem)

    @pl.when(j == 0)
    def _():
        x = x_ref[...]
        h1 = jnp.dot(x, s1_ref[...], preferred_element_type=F32)
        h3 = jnp.dot(x, s3_ref[...], preferred_element_type=F32)
        h_ref[...] = (jax.nn.silu(h1) * h3).astype(BF16)

    start_rows()
    shared = jnp.dot(h_ref[...], s2_ref[...], preferred_element_type=F32)
    gate = jax.nn.sigmoid(jnp.dot(x_ref[...], wg_ref[...], preferred_element_type=F32))
    emb = jnp.dot(p_ref[...], wp_ref[...], preferred_element_type=F32)
    o_ref[...] = (shared + gate * emb).astype(o_ref.dtype)
    wait_rows()


def _addend_dispatch(pad_start, pad_len, pos_tok_major, x1u, n_rows, x1b, p2, s1, s3, s2, wg, wp):
    T, D = x1b.shape
    _, S, L = x1u.shape
    H = s1.shape[1]
    PD = p2.shape[1]
    tn = _tile(D, 1024)
    tm = DISPATCH_TOKENS * (D // tn)
    assert T % tm == 0
    grid_spec = pltpu.PrefetchScalarGridSpec(
        num_scalar_prefetch=2,
        grid=(T // tm, D // tn),
        in_specs=[pl.BlockSpec(memory_space=pl.ANY),
                  pl.BlockSpec((tm, S, L), lambda i, j, ps, pn: (i, 0, 0)),
                  pl.BlockSpec((tm, D), lambda i, j, ps, pn: (i, 0)),
                  pl.BlockSpec((tm, PD), lambda i, j, ps, pn: (i, 0)),
                  pl.BlockSpec((D, H), lambda i, j, ps, pn: (0, 0)),
                  pl.BlockSpec((D, H), lambda i, j, ps, pn: (0, 0)),
                  pl.BlockSpec((H, tn), lambda i, j, ps, pn: (0, j)),
                  pl.BlockSpec((D, tn), lambda i, j, ps, pn: (0, j)),
                  pl.BlockSpec((PD, tn), lambda i, j, ps, pn: (0, j))],
        out_specs=[pl.BlockSpec((tm, tn), lambda i, j, ps, pn: (i, j)),
                   pl.BlockSpec(memory_space=pl.ANY)],
        scratch_shapes=[pltpu.VMEM((tm, H), BF16), pltpu.SMEM((SMEM_INDEX_CHUNK,), I32),
                        pltpu.SemaphoreType.DMA(()), pltpu.SemaphoreType.DMA(()),
                        pltpu.SemaphoreType.DMA(())],
    )
    return pl.pallas_call(
        _addend_kernel,
        grid_spec=grid_spec,
        out_shape=[jax.ShapeDtypeStruct((T, D), BF16), jax.ShapeDtypeStruct((n_rows, S, L), x1u.dtype)],
        compiler_params=_params("arbitrary", "arbitrary"),
        name="shared_ple_dispatch",
    )(pad_start, pad_len, pos_tok_major, x1u, x1b, p2, s1, s3, s2, wg, wp)


def _final_kernel(pos_hbm, y_hbm, x1_ref, add_ref, w_ref, g_ref, b_ref, o_ref,
                  buf_a, buf_b, idx_smem, idx_sem, row_sems, *, alpha):
    i = pl.program_id(0)
    last = pl.num_programs(0) - 1
    tm = x1_ref.shape[0]
    n = tm * TOP_K
    bufs = (buf_a, buf_b)

    def row_copy(slot, r, src_row):
        return pltpu.make_async_copy(y_hbm.at[src_row], bufs[slot].at[r], row_sems.at[slot])

    def fetch(step, slot, unroll):
        base = pl.multiple_of(step * n, SMEM_INDEX_CHUNK)
        idx_cp = pltpu.make_async_copy(pos_hbm.at[pl.ds(base, n)], idx_smem, idx_sem)
        idx_cp.start()
        idx_cp.wait()

        def issue(q, c):
            for d in range(DMA_QUEUES):
                r = q * DMA_QUEUES + d
                row_copy(slot, r, idx_smem[r]).start(priority=d)
            return c

        lax.fori_loop(0, n // DMA_QUEUES, issue, 0, unroll=unroll)

    def drain(slot):
        def one(r, c):
            row_copy(slot, 0, 0).wait()
            return c
        lax.fori_loop(0, n, one, 0, unroll=True)

    @pl.when(i == 0)
    def _():
        fetch(0, 0, DMA_LOOP_UNROLL)

    def step(slot):
        drain(slot)
        fetch(jnp.minimum(i + 1, last), 1 - slot, True)

        buf = bufs[slot]
        acc_lo = jnp.zeros((tm,) + buf.shape[1:], F32)
        acc_hi = acc_lo
        for k in range(TOP_K):
            lo, hi = _unpack_halves(buf[k * tm:(k + 1) * tm])
            wk = w_ref[:, k:k + 1, :]
            acc_lo += wk * lo
            acc_hi += wk * hi
        routed = jnp.concatenate([_rows_to_tiles(acc_lo), _rows_to_tiles(acc_hi)], axis=1)
        pre = alpha * x1_ref[...] + add_ref[...].astype(F32) + routed
        mu = jnp.mean(pre, axis=-1, keepdims=True)
        xc = pre - mu
        var = jnp.mean(xc * xc, axis=-1, keepdims=True)
        o_ref[...] = xc * lax.rsqrt(var + LN_EPS) * g_ref[...] + b_ref[...]

        @pl.when(i == last)
        def _():
            drain(1 - slot)

    for slot in range(2):
        pl.when(i % 2 == slot)(functools.partial(step, slot))


def _final(pos_flat, y_sorted, x1, addend, w_tok, g, b, alpha):
    T, D = x1.shape
    _, S, L = y_sorted.shape
    tm = SMEM_INDEX_CHUNK // TOP_K
    row = lambda i: (i, 0)
    const = lambda i: (0, 0)
    return pl.pallas_call(
        functools.partial(_final_kernel, alpha=alpha),
        grid=(T // tm,),
        in_specs=[pl.BlockSpec(memory_space=pl.ANY), pl.BlockSpec(memory_space=pl.ANY),
                  pl.BlockSpec((tm, D), row), pl.BlockSpec((tm, D), row),
                  pl.BlockSpec((tm, TOP_K, LANES), lambda i: (i, 0, 0)),
                  pl.BlockSpec((1, D), const), pl.BlockSpec((1, D), const)],
        out_specs=pl.BlockSpec((tm, D), row),
        out_shape=jax.ShapeDtypeStruct((T, D), F32),
        scratch_shapes=[pltpu.VMEM((tm * TOP_K, S, L), U32), pltpu.VMEM((tm * TOP_K, S, L), U32),
                        pltpu.SMEM((tm * TOP_K,), I32),
                        pltpu.SemaphoreType.DMA(()), pltpu.SemaphoreType.DMA((2,))],
        compiler_params=_params("arbitrary"),
        name="combine_ln2",
    )(pos_flat, y_sorted, x1, addend, w_tok, g, b)


def _dispatch_plan(top_idx, rank, counts, T):
    E = counts.shape[0]
    blk = EXPERT_BLOCK
    n_blocks = -(-(T * TOP_K) // blk) + E
    padded = (counts + blk - 1) // blk * blk
    ends = jnp.cumsum(padded)
    starts = ends - padded
    onehot = top_idx[None, :, :] == jnp.arange(E, dtype=I32)[:, None, None]
    pos = jnp.sum(jnp.where(onehot, starts[:, None, None], 0), axis=0) + rank
    first_row = jnp.arange(n_blocks, dtype=I32) * blk
    block_e = jnp.minimum(jnp.sum(ends[None, :] <= first_row[:, None], axis=1), E - 1).astype(I32)
    n_used = (ends[-1:] // blk).astype(I32)
    bidx = jnp.arange(n_blocks, dtype=I32)
    run_start = (bidx > 0) & (bidx < n_used[0]) & (block_e != jnp.roll(block_e, 1))
    nxt_start = lax.cummin(jnp.where(run_start, bidx, n_blocks), reverse=True)
    nxt_start = jnp.concatenate([nxt_start[1:], jnp.full((1,), n_blocks, I32)])
    next_e = jnp.where(nxt_start < n_blocks, block_e[jnp.minimum(nxt_start, n_blocks - 1)], -1)
    pad_start = jnp.concatenate([starts + counts, ends[-1:]])
    return pos, pad_start, padded - counts, block_e, next_e.astype(I32), n_used, n_blocks * blk


def kernel(x, p, w_in, pool_w, pool_scale, lb_param, hg_norm_w, w_branch_a, w_branch_b, w_out, ln1_g, ln1_b, router_w, router_bias, exp_w1, exp_w3, exp_w2, sh_w1, sh_w3, sh_w2, ple_gate_w, ple_proj_w, ln2_g, ln2_b):
    B, S, D = x.shape
    T = B * S
    depth = w_in.shape[0]
    alpha = (2.0 * depth) ** 0.25
    pool_width = w_branch_a.shape[1]
    hg_width = w_branch_b.shape[1]
    col_q = pool_width
    col_ga = pool_width + 4 * hg_width
    col_gb = col_ga + D
    lb_all = jnp.cumsum(jax.nn.softmax(lb_param.astype(F32), axis=0), axis=0)

    for i in range(depth):
        x2 = x.reshape(T, D)
        proj = _matmul(x2.astype(BF16), w_in[i].astype(BF16), BF16, 1024, 1024)
        o_b = _hgrn(proj.reshape(B, S, -1), lb_all[i][None, :], hg_norm_w[i][None, :].astype(F32),
                    col_q, hg_width).reshape(T, hg_width)
        mixed = _mix(proj, o_b, pool_w[i].astype(BF16), pool_scale[i][None, :].astype(F32),
                     w_branch_a[i].astype(BF16), w_branch_b[i].astype(BF16), S, col_ga, col_gb)
        pre = _outproj(mixed, w_out[i].astype(BF16), x2, alpha)

        rw_t = router_w[i].astype(F32).T
        rw_hi = rw_t.astype(BF16)
        rw_lo = (rw_t - rw_hi.astype(F32)).astype(BF16)
        x1, x1b, x1u, logits_t = _ln1(pre, ln1_g[i][None, :], ln1_b[i][None, :], rw_hi, rw_lo)

        top_idx, gate_w, rank, counts = _route(logits_t, router_bias[i].astype(F32)[:, None])
        pos, pad_start, pad_len, block_e, next_e, n_used, n_rows = _dispatch_plan(
            top_idx, rank, counts[:, 0].astype(I32), T)
        addend, xs = _addend_dispatch(
            pad_start, pad_len, pos.T.reshape(-1), x1u, n_rows,
            x1b, p[i].reshape(T, -1).astype(BF16), sh_w1[i].astype(BF16), sh_w3[i].astype(BF16),
            sh_w2[i].astype(BF16), ple_gate_w[i].astype(BF16), ple_proj_w[i].astype(BF16))
        ys = _experts(block_e, next_e, n_used, xs, exp_w1[i].astype(F32), exp_w3[i].astype(F32),
                      exp_w2[i].astype(F32))

        tm = SMEM_INDEX_CHUNK // TOP_K
        pos_flat = pos.reshape(TOP_K, T // tm, tm).transpose(1, 0, 2).reshape(-1)
        w_rep = jnp.broadcast_to(gate_w.T[:, :, None], (T, TOP_K, LANES))
        out = _final(pos_flat, ys, x1, addend, w_rep, ln2_g[i][None, :], ln2_b[i][None, :], alpha)
        x = out.reshape(B, S, D)
    return x
```

```python
import functools

import numpy as np
import jax
import jax.numpy as jnp
from jax import lax
from jax.experimental import pallas as pl
from jax.experimental.pallas import tpu as pltpu

F32 = jnp.float32
BF16 = jnp.bfloat16
U32 = jnp.uint32
I32 = jnp.int32

V7X_VMEM_BYTES = 64 * 1024 * 1024
VMEM_LIMIT = V7X_VMEM_BYTES - 12 * 1024 * 1024
EXPERT_VMEM_LIMIT = V7X_VMEM_BYTES - 6 * 1024 * 1024
LANES = 128
SMEM_INDEX_CHUNK = 1024

POOL_WINDOWS = (2, 4, 8, 16)
HG_DIM = 128
N_GROUPS = 8
TOPK_GROUPS = 4
TOP_K = 8
ROUTED_SCALE = 2.5
LN_EPS = 1e-5
RMS_EPS = 1e-6

HG_CHUNK = 64
HG_HEADS_PER_STEP = 8
EXPERT_BLOCK = 256
POOL_HALO = 16
DMA_LOOP_UNROLL = 8
DMA_QUEUES = 2


def _tile(n, pref):
    t = min(n, pref)
    while n % t:
        t -= 1
    return t


def _params(*sem):
    return pltpu.CompilerParams(dimension_semantics=sem, vmem_limit_bytes=VMEM_LIMIT)


def _mm_kernel(a_ref, b_ref, o_ref):
    o_ref[...] = jnp.dot(a_ref[...], b_ref[...], preferred_element_type=F32).astype(o_ref.dtype)


def _matmul(a, b, out_dtype, tm, tn):
    M, K = a.shape
    N = b.shape[1]
    tm, tn = _tile(M, tm), _tile(N, tn)
    return pl.pallas_call(
        _mm_kernel,
        grid=(M // tm, N // tn),
        in_specs=[pl.BlockSpec((tm, K), lambda i, j: (i, 0)),
                  pl.BlockSpec((K, tn), lambda i, j: (0, j))],
        out_specs=pl.BlockSpec((tm, tn), lambda i, j: (i, j)),
        out_shape=jax.ShapeDtypeStruct((M, N), out_dtype),
        compiler_params=_params("parallel", "parallel"),
        name="in_proj",
    )(a, b)


def _hgrn_consts():
    C = HG_CHUNK
    t = np.arange(C)[:, None]
    u = np.arange(C)[None, :]
    blocks = [u <= t, u > t]
    masks = []
    h = C // 2
    while h >= 1:
        same = (t // (2 * h)) == (u // (2 * h))
        t2 = (t % (2 * h)) >= h
        u2 = (u % (2 * h)) >= h
        blocks.append(same & ((t2 & u2 & (u <= t)) | (~t2 & ~u2 & (u > t))))
        masks.append(same & t2 & ~u2)
        h //= 2
    masks.append(t == u)
    wall = np.concatenate(blocks, axis=0).astype(np.float32)
    mask = np.stack(masks).astype(np.float32)
    return jnp.asarray(wall, BF16), jnp.asarray(mask, F32)


def _hgrn_kernel(q_ref, f_ref, i_ref, og_ref, lb_ref, nw_ref, wall_ref, mask_ref,
                 o_ref, *st_refs, heads, nchunk, nlev):
    C = HG_CHUNK
    D = HG_DIM

    @pl.when(pl.program_id(2) == 0)
    def _():
        for st_ref in st_refs:
            st_ref[...] = jnp.zeros_like(st_ref)

    nt = (((1,), (1,)), ((), ()))
    tn = (((0,), (0,)), ((), ()))

    def chunk(c, carry):
        r0 = pl.multiple_of(c * C, C)
        rows = pl.ds(r0, C)
        hs = range(heads)
        cols = [slice(j * D, (j + 1) * D) for j in hs]
        qs, ks, vs, gcat = [], [], [], []
        for cs in cols:
            lb = lb_ref[:, cs]
            fl = f_ref[rows, cs].astype(F32)
            e = jnp.exp(-jnp.abs(fl))
            r = 1.0 / (1.0 + e)
            pos = fl >= 0
            sig_p = jnp.where(pos, r, e * r)
            sig_n = jnp.where(pos, e * r, r)
            g = jnp.log(lb + (1.0 - lb) * sig_p)
            ks.append((1.0 - lb) * sig_n)
            qs.append(jax.nn.silu(q_ref[rows, cs].astype(F32)))
            vs.append(i_ref[rows, cs])
            g_hi = g.astype(BF16)
            gcat += [g_hi, (g - g_hi.astype(F32)).astype(BF16)]
        e2 = jnp.dot(wall_ref[...], jnp.concatenate(gcat, axis=1), preferred_element_type=F32)
        exs = [e2[:, 2 * j * D:(2 * j + 1) * D] + e2[:, (2 * j + 1) * D:(2 * j + 2) * D] for j in hs]
        sts = [st_refs[j][...] for j in hs]
        o_inter = [lax.dot_general((qs[j] * jnp.exp(exs[j][0:C])).astype(BF16), sts[j].astype(BF16), nt,
                                   preferred_element_type=F32) for j in hs]
        a = [lax.dot_general(qs[j].astype(BF16), ks[j].astype(BF16), nt,
                             preferred_element_type=F32) * mask_ref[nlev] for j in hs]
        for l in range(nlev):
            for j in hs:
                xl = jnp.exp(exs[j][(2 + l) * C:(3 + l) * C])
                a[j] = a[j] + lax.dot_general((qs[j] * xl).astype(BF16), (ks[j] * xl).astype(BF16), nt,
                                              preferred_element_type=F32) * mask_ref[l]
        os_ = [o_inter[j] + jnp.dot(a[j].astype(BF16), vs[j], preferred_element_type=F32) for j in hs]
        for j in hs:
            k_dec = (ks[j] * jnp.exp(exs[j][C:2 * C])).astype(BF16)
            st_refs[j][...] = sts[j] * jnp.exp(exs[j][C - 1:C, :]) + lax.dot_general(
                vs[j], k_dec, tn, preferred_element_type=F32)
        for j, cs in enumerate(cols):
            o = os_[j]
            o = o * lax.rsqrt(jnp.mean(o * o, axis=-1, keepdims=True) + RMS_EPS)
            o = o * nw_ref[:, cs] * jax.nn.silu(og_ref[rows, cs].astype(F32))
            o_ref[rows, cs] = o.astype(o_ref.dtype)
        return carry

    lax.fori_loop(0, nchunk, chunk, 0)


def _hgrn(proj3, lb, norm_w, col_q, width):
    B, S, _ = proj3.shape
    hw = HG_DIM * HG_HEADS_PER_STEP
    ts = _tile(S, 1024)
    wall, mask = _hgrn_consts()
    nlev = mask.shape[0] - 1
    off = [(col_q + k * width) // hw for k in range(4)]

    def seg(k):
        return pl.BlockSpec((None, ts, hw), lambda b, h, s: (b, s, off[k] + h))

    kern = functools.partial(_hgrn_kernel, heads=HG_HEADS_PER_STEP, nchunk=ts // HG_CHUNK, nlev=nlev)
    return pl.pallas_call(
        kern,
        grid=(B, width // hw, S // ts),
        in_specs=[seg(0), seg(1), seg(2), seg(3),
                  pl.BlockSpec((1, hw), lambda b, h, s: (0, h)),
                  pl.BlockSpec((1, hw), lambda b, h, s: (0, h)),
                  pl.BlockSpec(wall.shape, lambda b, h, s: (0, 0)),
                  pl.BlockSpec(mask.shape, lambda b, h, s: (0, 0, 0))],
        out_specs=pl.BlockSpec((None, ts, hw), lambda b, h, s: (b, s, h)),
        out_shape=jax.ShapeDtypeStruct((B, S, width), BF16),
        scratch_shapes=[pltpu.VMEM((HG_DIM, HG_DIM), F32)] * HG_HEADS_PER_STEP,
        compiler_params=_params("parallel", "parallel", "arbitrary"),
        name="hgrn2",
    )(proj3, proj3, proj3, proj3, lb, norm_w, wall, mask)


def _mix_kernel(u_ref, o_ref, ga_ref, gb_ref, pw_ref, ps_ref, wa_ref, wb_ref,
                out_ref, pooled_ref, halo_ref, *, tiles_per_seq):
    i = pl.program_id(0)
    tm = u_ref.shape[0]
    gw = pw_ref.shape[1]

    @pl.when(pl.program_id(1) == 0)
    def _():
        @pl.when(i % tiles_per_seq == 0)
        def _():
            halo_ref[...] = jnp.zeros_like(halo_ref)

        u = u_ref[...].astype(F32)
        ext = jnp.concatenate([halo_ref[...], u], axis=0)
        halo_ref[...] = u[tm - POOL_HALO:, :]
        t1 = ((i % tiles_per_seq) * tm + 1 + lax.broadcasted_iota(I32, (tm, 1), 0)).astype(F32)
        for g, w in enumerate(POOL_WINDOWS):
            cs = slice(g * gw, (g + 1) * gw)
            s = ext[:, cs]
            shift = 1
            while shift < w:
                s = s + pltpu.roll(s, shift, 0)
                shift *= 2
            d = s[POOL_HALO:, :] * (1.0 / jnp.minimum(t1, float(w))) - u[:, cs]
            y = jnp.dot(d.astype(BF16), pw_ref[g], preferred_element_type=F32) * ps_ref[:, cs]
            pooled_ref[:, cs] = y.astype(BF16)

    ya = jnp.dot(pooled_ref[...], wa_ref[...], preferred_element_type=F32)
    yb = jnp.dot(o_ref[...], wb_ref[...], preferred_element_type=F32)
    mixed = (jax.nn.sigmoid(ga_ref[...].astype(F32)) * ya
             + jax.nn.sigmoid(gb_ref[...].astype(F32)) * yb)
    out_ref[...] = mixed.astype(out_ref.dtype)


def _mix(proj, o_b, pool_w, pool_scale, wa, wb, seq, col_ga, col_gb):
    T = proj.shape[0]
    pwid = wa.shape[0]
    D = wa.shape[1]
    tm = _tile(seq, 512)
    tn = _tile(D, 1024)
    kern = functools.partial(_mix_kernel, tiles_per_seq=seq // tm)
    return pl.pallas_call(
        kern,
        grid=(T // tm, D // tn),
        in_specs=[pl.BlockSpec((tm, pwid), lambda i, j: (i, 0)),
                  pl.BlockSpec((tm, o_b.shape[1]), lambda i, j: (i, 0)),
                  pl.BlockSpec((tm, tn), lambda i, j: (i, col_ga // tn + j)),
                  pl.BlockSpec((tm, tn), lambda i, j: (i, col_gb // tn + j)),
                  pl.BlockSpec(pool_w.shape, lambda i, j: (0, 0, 0)),
                  pl.BlockSpec((1, pwid), lambda i, j: (0, 0)),
                  pl.BlockSpec((pwid, tn), lambda i, j: (0, j)),
                  pl.BlockSpec((wb.shape[0], tn), lambda i, j: (0, j))],
        out_specs=pl.BlockSpec((tm, tn), lambda i, j: (i, j)),
        out_shape=jax.ShapeDtypeStruct((T, D), BF16),
        scratch_shapes=[pltpu.VMEM((tm, pwid), BF16), pltpu.VMEM((POOL_HALO, pwid), F32)],
        compiler_params=_params("arbitrary", "arbitrary"),
        name="pool_mix",
    )(proj, o_b, proj, proj, pool_w, pool_scale, wa, wb)


def _outproj_kernel(a_ref, w_ref, x_ref, o_ref, *, alpha):
    o_ref[...] = alpha * x_ref[...] + jnp.dot(a_ref[...], w_ref[...], preferred_element_type=F32)


def _outproj(mixed, w_out, x2, alpha):
    T, D = x2.shape
    tm, tn = _tile(T, 512), _tile(D, 1024)
    return pl.pallas_call(
        functools.partial(_outproj_kernel, alpha=alpha),
        grid=(T // tm, D // tn),
        in_specs=[pl.BlockSpec((tm, D), lambda i, j: (i, 0)),
                  pl.BlockSpec((D, tn), lambda i, j: (0, j)),
                  pl.BlockSpec((tm, tn), lambda i, j: (i, j))],
        out_specs=pl.BlockSpec((tm, tn), lambda i, j: (i, j)),
        out_shape=jax.ShapeDtypeStruct((T, D), F32),
        compiler_params=_params("parallel", "parallel"),
        name="out_proj",
    )(mixed, w_out, x2)


def _pack_halves(x):
    n = x.shape[1] // 2
    lo = pltpu.bitcast(x[:, :n].astype(BF16).astype(F32), U32)
    hi = pltpu.bitcast(x[:, n:].astype(BF16).astype(F32), U32)
    return (lo >> 16) | (hi & jnp.uint32(0xFFFF0000))


def _unpack_halves(u):
    lo = pltpu.bitcast(u << 16, F32)
    hi = pltpu.bitcast(u & jnp.uint32(0xFFFF0000), F32)
    return lo, hi


SUBLANES = 8


def _exchange(a, inner):
    n = a.shape[0]
    s_idx = lax.broadcasted_iota(I32, (1, 1, SUBLANES, LANES), 2)
    d = SUBLANES // 2
    while d >= 1:
        span = d * inner
        a4 = a.reshape(n // (2 * span), 2, span, SUBLANES, LANES)
        lo, hi = a4[:, 0], a4[:, 1]
        keep = (s_idx & d) == 0
        new_lo = jnp.where(keep, lo, pltpu.roll(hi, d, 2))
        new_hi = jnp.where(keep, pltpu.roll(lo, SUBLANES - d, 2), hi)
        a = jnp.stack([new_lo, new_hi], axis=1).reshape(n, SUBLANES, LANES)
        d //= 2
    return a


def _rows_to_tiles(x3):
    r, c, _ = x3.shape
    inner = c // SUBLANES
    a = _exchange(x3.reshape(r * inner, SUBLANES, LANES), inner)
    a = a.reshape(r // SUBLANES, SUBLANES, inner, SUBLANES, LANES)
    return jnp.concatenate([a[:, ci % SUBLANES, ci // SUBLANES].reshape(r, LANES) for ci in range(c)],
                           axis=1)


def _tiles_to_rows(x, c):
    r = x.shape[0]
    inner = c // SUBLANES
    g = r // SUBLANES
    cols = [x[:, ci * LANES:(ci + 1) * LANES].reshape(g, 1, 1, SUBLANES, LANES) for ci in range(c)]
    a = jnp.concatenate(
        [jnp.concatenate([cols[hi * SUBLANES + lo] for hi in range(inner)], axis=2)
         for lo in range(SUBLANES)], axis=1)
    a = _exchange(a.reshape(r * inner, SUBLANES, LANES), inner)
    return a.reshape(r, c, LANES)


def _ln1_kernel(pre_ref, g_ref, b_ref, rwh_ref, rwl_ref, x1_ref, x1b_ref, x1u_ref, lg_ref):
    x = pre_ref[...]
    mu = jnp.mean(x, axis=-1, keepdims=True)
    xc = x - mu
    var = jnp.mean(xc * xc, axis=-1, keepdims=True)
    y = xc * lax.rsqrt(var + LN_EPS) * g_ref[...] + b_ref[...]
    x1_ref[...] = y
    y_hi = y.astype(BF16)
    x1b_ref[...] = y_hi
    x1u_ref[...] = _tiles_to_rows(_pack_halves(y), x1u_ref.shape[1])
    y_lo = (y - y_hi.astype(F32)).astype(BF16)
    nt = (((1,), (1,)), ((), ()))
    lg = lax.dot_general(rwh_ref[...], y_hi, nt, preferred_element_type=F32)
    lg += lax.dot_general(rwh_ref[...], y_lo, nt, preferred_element_type=F32)
    lg += lax.dot_general(rwl_ref[...], y_hi, nt, preferred_element_type=F32)
    lg_ref[...] = lg


def _ln1(pre, g, b, rw_hi_t, rw_lo_t):
    T, D = pre.shape
    E = rw_hi_t.shape[0]
    tm = _tile(T, 256)
    row = lambda i: (i, 0)
    const = lambda i: (0, 0)
    return pl.pallas_call(
        _ln1_kernel,
        grid=(T // tm,),
        in_specs=[pl.BlockSpec((tm, D), row), pl.BlockSpec((1, D), const), pl.BlockSpec((1, D), const),
                  pl.BlockSpec((E, D), const), pl.BlockSpec((E, D), const)],
        out_specs=[pl.BlockSpec((tm, D), row), pl.BlockSpec((tm, D), row),
                   pl.BlockSpec((tm, D // 2 // LANES, LANES), lambda i: (i, 0, 0)),
                   pl.BlockSpec((E, tm), lambda i: (0, i))],
        out_shape=[jax.ShapeDtypeStruct((T, D), F32), jax.ShapeDtypeStruct((T, D), BF16),
                   jax.ShapeDtypeStruct((T, D // 2 // LANES, LANES), U32),
                   jax.ShapeDtypeStruct((E, T), F32)],
        compiler_params=_params("parallel"),
        name="ln1_router",
    )(pre, g, b, rw_hi_t, rw_lo_t)


def _route_kernel(lg_ref, bias_ref, tri_ref, idx_ref, w_ref, rank_ref, cnt_ref, carry_ref):
    @pl.when(pl.program_id(0) == 0)
    def _():
        carry_ref[...] = jnp.zeros_like(carry_ref)

    E, tr = lg_ref.shape
    gsz = E // N_GROUPS
    ninf = -jnp.inf
    scores = jax.nn.sigmoid(lg_ref[...])
    choice = scores + bias_ref[...]

    def first_max(x, n):
        io = lax.broadcasted_iota(I32, x.shape, 0)
        m = jnp.max(x, axis=0, keepdims=True)
        return m, jnp.min(jnp.where(x == m, io, n), axis=0, keepdims=True), io

    gs_rows = []
    for g in range(N_GROUPS):
        cg = choice[g * gsz:(g + 1) * gsz, :]
        m1, i1, io = first_max(cg, gsz)
        m2 = jnp.max(jnp.where(io == i1, ninf, cg), axis=0, keepdims=True)
        gs_rows.append(m1 + m2)
    gs = jnp.concatenate(gs_rows, axis=0)
    gsel = jnp.zeros(gs.shape, F32)
    for _ in range(TOPK_GROUPS):
        _, gi, io = first_max(gs, N_GROUPS)
        hit = io == gi
        gsel = jnp.where(hit, 1.0, gsel)
        gs = jnp.where(hit, ninf, gs)
    gsel_e = jnp.concatenate(
        [jnp.broadcast_to(gsel[g:g + 1, :], (gsz, tr)) for g in range(N_GROUPS)], axis=0)
    masked = jnp.where(gsel_e > 0.0, choice, ninf)

    idx_rows, w_rows = [], []
    member = jnp.zeros((E, tr), F32)
    for _ in range(TOP_K):
        _, ii, io = first_max(masked, E)
        hit = io == ii
        idx_rows.append(ii)
        w_rows.append(jnp.sum(jnp.where(hit, scores, 0.0), axis=0, keepdims=True))
        member = jnp.where(hit, 1.0, member)
        masked = jnp.where(hit, ninf, masked)
    wsum = w_rows[0]
    for wk in w_rows[1:]:
        wsum = wsum + wk
    idx_ref[...] = jnp.concatenate(idx_rows, axis=0)
    w_ref[...] = jnp.concatenate([wk / wsum * ROUTED_SCALE for wk in w_rows], axis=0)

    before = jnp.dot(member.astype(BF16), tri_ref[...], preferred_element_type=F32) + carry_ref[...]
    io = lax.broadcasted_iota(I32, (E, tr), 0)
    rank_ref[...] = jnp.concatenate(
        [jnp.sum(jnp.where(io == ii, before, 0.0), axis=0, keepdims=True) for ii in idx_rows],
        axis=0).astype(I32)
    carry_ref[...] += jnp.sum(member, axis=1, keepdims=True)
    cnt_ref[...] = carry_ref[...]


def _route(logits_t, bias):
    E, T = logits_t.shape
    tr = _tile(T, 512)
    tri = jnp.asarray(np.triu(np.ones((tr, tr), np.float32), k=1), BF16)
    col = lambda i: (0, i)
    return pl.pallas_call(
        _route_kernel,
        grid=(T // tr,),
        in_specs=[pl.BlockSpec((E, tr), col), pl.BlockSpec((E, 1), lambda i: (0, 0)),
                  pl.BlockSpec((tr, tr), lambda i: (0, 0))],
        out_specs=[pl.BlockSpec((TOP_K, tr), col), pl.BlockSpec((TOP_K, tr), col),
                   pl.BlockSpec((TOP_K, tr), col), pl.BlockSpec((E, 1), lambda i: (0, 0))],
        out_shape=[jax.ShapeDtypeStruct((TOP_K, T), I32), jax.ShapeDtypeStruct((TOP_K, T), F32),
                   jax.ShapeDtypeStruct((TOP_K, T), I32), jax.ShapeDtypeStruct((E, 1), F32)],
        scratch_shapes=[pltpu.VMEM((E, 1), F32)],
        compiler_params=_params("arbitrary"),
        name="route",
    )(logits_t, bias, tri)


DISPATCH_TOKENS = SMEM_INDEX_CHUNK // TOP_K
PAD_SIZES = tuple(1 << s for s in reversed(range((EXPERT_BLOCK - 1).bit_length())))
assert PAD_SIZES[0] <= DISPATCH_TOKENS


def _dispatch_rows(step, tok0, pad_start_ref, pad_len_ref, pos_hbm, x_ref, xs_hbm,
                   idx_smem, idx_sem, row_sem, pad_sem):
    n = idx_smem.shape[0]
    base = pl.multiple_of(step * n, SMEM_INDEX_CHUNK)
    idx_cp = pltpu.make_async_copy(pos_hbm.at[pl.ds(base, n)], idx_smem, idx_sem)
    idx_cp.start()
    idx_cp.wait()

    def row_copy(t, dst_row):
        return pltpu.make_async_copy(x_ref.at[t], xs_hbm.at[dst_row], row_sem)

    def issue(t, c):
        for k in range(TOP_K):
            row_copy(tok0 + t, idx_smem[t * TOP_K + k]).start(priority=k % DMA_QUEUES)
        return c

    def drain(r, c):
        row_copy(0, 0).wait()
        return c

    @pl.when(step == 0)
    def _():
        def pad_pass(act):
            def per_expert(e, c):
                ln = pad_len_ref[e]
                off = pad_start_ref[e]
                for sz in PAD_SIZES:
                    @pl.when((ln & sz) != 0)
                    def _():
                        dst = off + (ln & ~(2 * sz - 1))
                        act(pltpu.make_async_copy(x_ref.at[pl.ds(0, sz)], xs_hbm.at[pl.ds(dst, sz)],
                                                  pad_sem))
                return c
            lax.fori_loop(0, pad_len_ref.shape[0], per_expert, 0)

            run = PAD_SIZES[0]
            first = pad_start_ref[pad_len_ref.shape[0]]

            def per_run(q, c):
                act(pltpu.make_async_copy(x_ref.at[pl.ds(0, run)],
                                          xs_hbm.at[pl.ds(first + q * run, run)], pad_sem))
                return c
            lax.fori_loop(0, (xs_hbm.shape[0] - first) // run, per_run, 0)

        pad_pass(lambda cp: cp.start())
        pad_pass(lambda cp: cp.wait())

    return (lambda: lax.fori_loop(0, n // TOP_K, issue, 0),
            lambda: lax.fori_loop(0, n, drain, 0, unroll=True))


def _expert_kernel(be_ref, nxt_ref, nu_ref, xs_ref, w1_hbm, w3_hbm, w2_hbm, y_ref,
                   w1f, w3f, w2f, w1b, w3b, w2b, slot_ref, wsem):
    n = pl.program_id(0)
    used = n < nu_ref[0]

    @pl.when(jnp.logical_not(used))
    def _():
        y_ref[...] = jnp.zeros_like(y_ref)

    def weight_copies(e, s):
        return [pltpu.make_async_copy(src.at[e], dst.at[s], wsem.at[s])
                for src, dst in ((w1_hbm, w1f), (w3_hbm, w3f), (w2_hbm, w2f))]

    @pl.when(used)
    def _():
        prev = be_ref[jnp.maximum(n - 1, 0)]

        @pl.when((n == 0) | (be_ref[n] != prev))
        def _():
            @pl.when(n == 0)
            def _():
                slot_ref[0] = 1
                for cp in weight_copies(be_ref[0], 0):
                    cp.start()

            s = 1 - slot_ref[0]
            slot_ref[0] = s
            for cp in weight_copies(0, s):
                cp.wait()

            @pl.when(nxt_ref[n] >= 0)
            def _():
                for cp in weight_copies(nxt_ref[n], 1 - s):
                    cp.start(priority=1)

            w1b[...] = w1f[s].astype(BF16)
            w3b[...] = w3f[s].astype(BF16)
            w2b[...] = w2f[s].astype(BF16)

        lo, hi = _unpack_halves(_rows_to_tiles(xs_ref[...]))
        half = lo.shape[1]
        lo, hi = lo.astype(BF16), hi.astype(BF16)
        h1 = (jnp.dot(lo, w1b[:half, :], preferred_element_type=F32)
              + jnp.dot(hi, w1b[half:, :], preferred_element_type=F32))
        h3 = (jnp.dot(lo, w3b[:half, :], preferred_element_type=F32)
              + jnp.dot(hi, w3b[half:, :], preferred_element_type=F32))
        hdn = (jax.nn.silu(h1) * h3).astype(BF16)
        y = _pack_halves(jnp.dot(hdn, w2b[...], preferred_element_type=F32))
        y_ref[...] = _tiles_to_rows(y, y_ref.shape[1])


def _experts(block_e, next_e, n_used, xs, w1, w3, w2):
    P, S, L = xs.shape
    E, D, H = w1.shape
    nb = P // EXPERT_BLOCK
    grid_spec = pltpu.PrefetchScalarGridSpec(
        num_scalar_prefetch=3,
        grid=(nb,),
        in_specs=[pl.BlockSpec((EXPERT_BLOCK, S, L),
                               lambda n, be, nx, nu: (jnp.minimum(n, nu[0] - 1), 0, 0)),
                  pl.BlockSpec(memory_space=pl.ANY), pl.BlockSpec(memory_space=pl.ANY),
                  pl.BlockSpec(memory_space=pl.ANY)],
        out_specs=pl.BlockSpec((EXPERT_BLOCK, S, L), lambda n, be, nx, nu: (n, 0, 0)),
        scratch_shapes=[pltpu.VMEM((2, D, H), F32), pltpu.VMEM((2, D, H), F32), pltpu.VMEM((2, H, D), F32),
                        pltpu.VMEM((D, H), BF16), pltpu.VMEM((D, H), BF16), pltpu.VMEM((H, D), BF16),
                        pltpu.SMEM((1,), I32), pltpu.SemaphoreType.DMA((2,))],
    )
    return pl.pallas_call(
        _expert_kernel,
        grid_spec=grid_spec,
        out_shape=jax.ShapeDtypeStruct((P, S, L), U32),
        compiler_params=pltpu.CompilerParams(dimension_semantics=("arbitrary",),
                                             vmem_limit_bytes=EXPERT_VMEM_LIMIT),
        name="experts",
    )(block_e, next_e, n_used, xs, w1, w3, w2)


def _addend_kernel(pad_start_ref, pad_len_ref, pos_hbm, xu_ref, x_ref, p_ref, s1_ref, s3_ref, s2_ref,
                   wg_ref, wp_ref, o_ref, xs_hbm, h_ref, idx_smem, idx_sem, row_sem, pad_sem):
    j = pl.program_id(0)
    i = pl.program_id(1)
    tm = x_ref.shape[0]
    start_rows, wait_rows = _dispatch_rows(i * pl.num_programs(0) + j, 0, pad_start_ref, pad_len_ref,
                                           pos_hbm, xu_ref, xs_hbm, idx_smem, idx_sem, row_sem, pad_sem)
    rows = pl.ds(pl.multiple_of(i * tm, tm), tm)

    @pl.when(j == 0)
    def _():
        x = x_ref[...]
        h1 = jnp.dot(x, s1_ref[...], preferred_element_type=F32)
        h3 = jnp.dot(x, s3_ref[...], preferred_element_type=F32)
        h_ref[rows, :] = (jax.nn.silu(h1) * h3).astype(BF16)

    start_rows()
    shared = jnp.dot(h_ref[rows, :], s2_ref[...], preferred_element_type=F32)
    gate = jax.nn.sigmoid(jnp.dot(x_ref[...], wg_ref[...], preferred_element_type=F32))
    emb = jnp.dot(p_ref[...], wp_ref[...], preferred_element_type=F32)
    o_ref[...] = (shared + gate * emb).astype(o_ref.dtype)
    wait_rows()


def _addend_dispatch(pad_start, pad_len, pos_tok_major, x1u, n_rows, x1b, p2, s1, s3, s2, wg, wp):
    T, D = x1b.shape
    _, S, L = x1u.shape
    H = s1.shape[1]
    PD = p2.shape[1]
    tn = _tile(D, 1024)
    nj = D // tn
    tm = DISPATCH_TOKENS * nj
    assert T % tm == 0
    grid_spec = pltpu.PrefetchScalarGridSpec(
        num_scalar_prefetch=2,
        grid=(nj, T // tm),
        in_specs=[pl.BlockSpec(memory_space=pl.ANY),
                  pl.BlockSpec((DISPATCH_TOKENS, S, L), lambda j, i, ps, pn: (i * nj + j, 0, 0)),
                  pl.BlockSpec((tm, D), lambda j, i, ps, pn: (i, 0)),
                  pl.BlockSpec((tm, PD), lambda j, i, ps, pn: (i, 0)),
                  pl.BlockSpec((D, H), lambda j, i, ps, pn: (0, 0)),
                  pl.BlockSpec((D, H), lambda j, i, ps, pn: (0, 0)),
                  pl.BlockSpec((H, tn), lambda j, i, ps, pn: (0, j)),
                  pl.BlockSpec((D, tn), lambda j, i, ps, pn: (0, j)),
                  pl.BlockSpec((PD, tn), lambda j, i, ps, pn: (0, j))],
        out_specs=[pl.BlockSpec((tm, tn), lambda j, i, ps, pn: (i, j)),
                   pl.BlockSpec(memory_space=pl.ANY)],
        scratch_shapes=[pltpu.VMEM((T, H), BF16), pltpu.SMEM((SMEM_INDEX_CHUNK,), I32),
                        pltpu.SemaphoreType.DMA(()), pltpu.SemaphoreType.DMA(()),
                        pltpu.SemaphoreType.DMA(())],
    )
    return pl.pallas_call(
        _addend_kernel,
        grid_spec=grid_spec,
        out_shape=[jax.ShapeDtypeStruct((T, D), BF16), jax.ShapeDtypeStruct((n_rows, S, L), x1u.dtype)],
        compiler_params=_params("arbitrary", "arbitrary"),
        name="shared_ple_dispatch",
    )(pad_start, pad_len, pos_tok_major, x1u, x1b, p2, s1, s3, s2, wg, wp)


def _final_kernel(pos_hbm, y_hbm, x1_ref, add_ref, w_ref, g_ref, b_ref, o_ref,
                  buf_a, buf_b, idx_smem, idx_sem, row_sems, *, alpha):
    i = pl.program_id(0)
    last = pl.num_programs(0) - 1
    tm = x1_ref.shape[0]
    n = tm * TOP_K
    bufs = (buf_a, buf_b)

    def row_copy(slot, r, src_row):
        return pltpu.make_async_copy(y_hbm.at[src_row], bufs[slot].at[r], row_sems.at[slot])

    def fetch(step, slot, unroll):
        base = pl.multiple_of(step * n, SMEM_INDEX_CHUNK)
        idx_cp = pltpu.make_async_copy(pos_hbm.at[pl.ds(base, n)], idx_smem, idx_sem)
        idx_cp.start()
        idx_cp.wait()

        def issue(r, c):
            row_copy(slot, r, idx_smem[r]).start()
            return c

        lax.fori_loop(0, n, issue, 0, unroll=unroll)

    def drain(slot):
        def one(r, c):
            row_copy(slot, 0, 0).wait()
            return c
        lax.fori_loop(0, n, one, 0, unroll=True)

    @pl.when(i == 0)
    def _():
        fetch(0, 0, DMA_LOOP_UNROLL)

    def step(slot):
        drain(slot)
        fetch(jnp.minimum(i + 1, last), 1 - slot, True)

        buf = bufs[slot]
        acc_lo = jnp.zeros((tm,) + buf.shape[1:], F32)
        acc_hi = acc_lo
        for k in range(TOP_K):
            lo, hi = _unpack_halves(buf[k * tm:(k + 1) * tm])
            wk = w_ref[:, k:k + 1, :]
            acc_lo += wk * lo
            acc_hi += wk * hi
        routed = jnp.concatenate([_rows_to_tiles(acc_lo), _rows_to_tiles(acc_hi)], axis=1)
        pre = alpha * x1_ref[...] + add_ref[...].astype(F32) + routed
        mu = jnp.mean(pre, axis=-1, keepdims=True)
        xc = pre - mu
        var = jnp.mean(xc * xc, axis=-1, keepdims=True)
        o_ref[...] = xc * lax.rsqrt(var + LN_EPS) * g_ref[...] + b_ref[...]

        @pl.when(i == last)
        def _():
            drain(1 - slot)

    for slot in range(2):
        pl.when(i % 2 == slot)(functools.partial(step, slot))


def _final(pos_flat, y_sorted, x1, addend, w_tok, g, b, alpha):
    T, D = x1.shape
    _, S, L = y_sorted.shape
    tm = SMEM_INDEX_CHUNK // TOP_K
    row = lambda i: (i, 0)
    const = lambda i: (0, 0)
    return pl.pallas_call(
        functools.partial(_final_kernel, alpha=alpha),
        grid=(T // tm,),
        in_specs=[pl.BlockSpec(memory_space=pl.ANY), pl.BlockSpec(memory_space=pl.ANY),
                  pl.BlockSpec((tm, D), row), pl.BlockSpec((tm, D), row),
                  pl.BlockSpec((tm, TOP_K, LANES), lambda i: (i, 0, 0)),
                  pl.BlockSpec((1, D), const), pl.BlockSpec((1, D), const)],
        out_specs=pl.BlockSpec((tm, D), row),
        out_shape=jax.ShapeDtypeStruct((T, D), F32),
        scratch_shapes=[pltpu.VMEM((tm * TOP_K, S, L), U32), pltpu.VMEM((tm * TOP_K, S, L), U32),
                        pltpu.SMEM((tm * TOP_K,), I32),
                        pltpu.SemaphoreType.DMA(()), pltpu.SemaphoreType.DMA((2,))],
        compiler_params=_params("arbitrary"),
        name="combine_ln2",
    )(pos_flat, y_sorted, x1, addend, w_tok, g, b)


def _dispatch_plan(top_idx, rank, counts, T):
    E = counts.shape[0]
    blk = EXPERT_BLOCK
    n_blocks = -(-(T * TOP_K) // blk) + E
    padded = (counts + blk - 1) // blk * blk
    ends = jnp.cumsum(padded)
    starts = ends - padded
    onehot = top_idx[None, :, :] == jnp.arange(E, dtype=I32)[:, None, None]
    pos = jnp.sum(jnp.where(onehot, starts[:, None, None], 0), axis=0) + rank
    first_row = jnp.arange(n_blocks, dtype=I32) * blk
    block_e = jnp.minimum(jnp.sum(ends[None, :] <= first_row[:, None], axis=1), E - 1).astype(I32)
    n_used = (ends[-1:] // blk).astype(I32)
    bidx = jnp.arange(n_blocks, dtype=I32)
    run_start = (bidx > 0) & (bidx < n_used[0]) & (block_e != jnp.roll(block_e, 1))
    nxt_start = lax.cummin(jnp.where(run_start, bidx, n_blocks), reverse=True)
    nxt_start = jnp.concatenate([nxt_start[1:], jnp.full((1,), n_blocks, I32)])
    next_e = jnp.where(nxt_start < n_blocks, block_e[jnp.minimum(nxt_start, n_blocks - 1)], -1)
    pad_start = jnp.concatenate([starts + counts, ends[-1:]])
    return pos, pad_start, padded - counts, block_e, next_e.astype(I32), n_used, n_blocks * blk


def kernel(x, p, w_in, pool_w, pool_scale, lb_param, hg_norm_w, w_branch_a, w_branch_b, w_out, ln1_g, ln1_b, router_w, router_bias, exp_w1, exp_w3, exp_w2, sh_w1, sh_w3, sh_w2, ple_gate_w, ple_proj_w, ln2_g, ln2_b):
    B, S, D = x.shape
    T = B * S
    depth = w_in.shape[0]
    alpha = (2.0 * depth) ** 0.25
    pool_width = w_branch_a.shape[1]
    hg_width = w_branch_b.shape[1]
    col_q = pool_width
    col_ga = pool_width + 4 * hg_width
    col_gb = col_ga + D
    lb_all = jnp.cumsum(jax.nn.softmax(lb_param.astype(F32), axis=0), axis=0)

    for i in range(depth):
        x2 = x.reshape(T, D)
        proj = _matmul(x2.astype(BF16), w_in[i].astype(BF16), BF16, 1024, 1024)
        o_b = _hgrn(proj.reshape(B, S, -1), lb_all[i][None, :], hg_norm_w[i][None, :].astype(F32),
                    col_q, hg_width).reshape(T, hg_width)
        mixed = _mix(proj, o_b, pool_w[i].astype(BF16), pool_scale[i][None, :].astype(F32),
                     w_branch_a[i].astype(BF16), w_branch_b[i].astype(BF16), S, col_ga, col_gb)
        pre = _outproj(mixed, w_out[i].astype(BF16), x2, alpha)

        rw_t = router_w[i].astype(F32).T
        rw_hi = rw_t.astype(BF16)
        rw_lo = (rw_t - rw_hi.astype(F32)).astype(BF16)
        x1, x1b, x1u, logits_t = _ln1(pre, ln1_g[i][None, :], ln1_b[i][None, :], rw_hi, rw_lo)

        top_idx, gate_w, rank, counts = _route(logits_t, router_bias[i].astype(F32)[:, None])
        pos, pad_start, pad_len, block_e, next_e, n_used, n_rows = _dispatch_plan(
            top_idx, rank, counts[:, 0].astype(I32), T)
        addend, xs = _addend_dispatch(
            pad_start, pad_len, pos.T.reshape(-1), x1u, n_rows,
            x1b, p[i].reshape(T, -1).astype(BF16), sh_w1[i].astype(BF16), sh_w3[i].astype(BF16),
            sh_w2[i].astype(BF16), ple_gate_w[i].astype(BF16), ple_proj_w[i].astype(BF16))
        ys = _experts(block_e, next_e, n_used, xs, exp_w1[i].astype(F32), exp_w3[i].astype(F32),
                      exp_w2[i].astype(F32))

        tm = SMEM_INDEX_CHUNK // TOP_K
        pos_flat = pos.reshape(TOP_K, T // tm, tm).transpose(1, 0, 2).reshape(-1)
        w_rep = jnp.broadcast_to(gate_w.T[:, :, None], (T, TOP_K, LANES))
        out = _final(pos_flat, ys, x1, addend, w_rep, ln2_g[i][None, :], ln2_b[i][None, :], alpha)
        x = out.reshape(B, S, D)
    return x
```
